```python
import math
import jax, jax.numpy as jnp
from jax import lax
import numpy as np

D_MODEL = 1024
BATCH = 32
SEQ = 2048
DEPTH = 1

MEM_LEN = 256
D_MIX = D_MODEL
HEAD_DIM = 64
ATTN_HEADS = 8
D_ATTN = ATTN_HEADS * HEAD_DIM
CONV_GROUPS = 8
D_CONV = D_MIX - D_ATTN
CONV_WIDTH = 3
MOBA_BLOCK = 256
MOBA_TOPK = 3
Q_CHUNK = 128
MEM_HEADS = 4
MEM_HEAD_DIM = D_MODEL // MEM_HEADS
D_FF = ((8 * D_MODEL // 3) + 127) // 128 * 128
FFN_CONV_WIDTH = 3
EPS = 1e-6
D_IN_MIX = 3 * D_ATTN + 3 * D_CONV

kernel_name = "hymba_moba_shortconv_convffn_layer"


def rmsnorm(x, g):
    xf = x.astype(jnp.float32)
    r = lax.rsqrt(jnp.mean(xf * xf, axis=-1, keepdims=True) + EPS)
    return (xf * r).astype(x.dtype) * g


def causal_dwconv(x, w):
    K = w.shape[0]
    S = x.shape[1]
    xp = jnp.pad(x, ((0, 0), (K - 1, 0), (0, 0)))
    y = xp[:, 0:S] * w[0]
    for k in range(1, K):
        y = y + xp[:, k:k + S] * w[k]
    return y


def moba_attention(q, k, v):
    B, S, H, Dh = q.shape
    L = MOBA_BLOCK
    nb = -(-S // L)
    pad = nb * L - S
    topk = max(1, min(MOBA_TOPK, nb - 1))
    scale = Dh ** -0.5
    kp = jnp.pad(k, ((0, 0), (0, pad), (0, 0), (0, 0)))
    vp = jnp.pad(v, ((0, 0), (0, pad), (0, 0), (0, 0)))
    kb = kp.reshape(B, nb, L, H, Dh).transpose(0, 3, 1, 2, 4)
    vb = vp.reshape(B, nb, L, H, Dh).transpose(0, 3, 1, 2, 4)
    kmean = jnp.mean(kb.astype(jnp.float32), axis=3)

    pos = jnp.arange(S)
    qblk = pos // L
    gate = jnp.einsum('bshd,bhnd->bhsn', q.astype(jnp.float32), kmean)
    past = jnp.arange(nb)[None, :] < qblk[:, None]
    gate = jnp.where(past[None, None], gate, -jnp.inf)
    _, sel = lax.top_k(gate, topk)

    nc = S // Q_CHUNK
    qc = q.reshape(B, nc, Q_CHUNK, H, Dh).transpose(0, 1, 3, 2, 4).reshape(B * nc, H, Q_CHUNK, Dh)
    selc = sel.reshape(B, H, nc, Q_CHUNK, topk).transpose(0, 2, 1, 3, 4).reshape(B * nc, H, Q_CHUNK, topk)

    def one_chunk(args):
        i, q_c, sel_c = args
        b = i // nc
        c = i % nc
        k_b = kb[b]
        v_b = vb[b]
        qpos = c * Q_CHUNK + jnp.arange(Q_CHUNK)
        own = (c * Q_CHUNK) // L
        k_sel = jax.vmap(lambda kh, sh: kh[sh])(k_b, sel_c)
        v_sel = jax.vmap(lambda vh, sh: vh[sh])(v_b, sel_c)
        s_sel = jnp.einsum('hcd,hcjld->hcjl', q_c, k_sel).astype(jnp.float32) * scale
        ok_sel = jnp.arange(topk)[None, :] < (qpos // L)[:, None]
        s_sel = jnp.where(ok_sel[None, :, :, None], s_sel, -jnp.inf)
        k_own = lax.dynamic_index_in_dim(k_b, own, axis=1, keepdims=False)
        v_own = lax.dynamic_index_in_dim(v_b, own, axis=1, keepdims=False)
        s_own = jnp.einsum('hcd,hld->hcl', q_c, k_own).astype(jnp.float32) * scale
        kpos = own * L + jnp.arange(L)
        s_own = jnp.where((kpos[None, :] <= qpos[:, None])[None], s_own, -jnp.inf)
        s = jnp.concatenate([s_sel.reshape(H, Q_CHUNK, topk * L), s_own], axis=-1)
        p = jax.nn.softmax(s, axis=-1).astype(v_b.dtype)
        p_sel = p[..., :topk * L].reshape(H, Q_CHUNK, topk, L)
        p_own = p[..., topk * L:]
        o = jnp.einsum('hcjl,hcjld->hcd', p_sel, v_sel) + jnp.einsum('hcl,hld->hcd', p_own, v_own)
        return o

    out = lax.map(one_chunk, (jnp.arange(B * nc, dtype=jnp.int32), qc, selc))
    return out.reshape(B, nc, H, Q_CHUNK, Dh).transpose(0, 1, 3, 2, 4).reshape(B, S, H, Dh)


def parallel_mixer(h, w_in, conv_w, attn_g, conv_g, w_out):
    B, S, _ = h.shape
    proj = h @ w_in
    o1 = D_ATTN
    o2 = 2 * D_ATTN
    o3 = 3 * D_ATTN
    o4 = o3 + D_CONV
    o5 = o4 + D_CONV
    q, k, v, bg, cg, u = jnp.split(proj, [o1, o2, o3, o4, o5], axis=-1)
    q = q.reshape(B, S, ATTN_HEADS, HEAD_DIM)
    k = k.reshape(B, S, ATTN_HEADS, HEAD_DIM)
    v = v.reshape(B, S, ATTN_HEADS, HEAD_DIM)
    y_attn = moba_attention(q, k, v).reshape(B, S, D_ATTN)
    y_conv = bg * causal_dwconv(cg * u, conv_w)
    y = jnp.concatenate([rmsnorm(y_attn, attn_g), rmsnorm(y_conv, conv_g)], axis=-1)
    return y @ w_out


def memory_cross_attention(h, m, w_q, w_kv, w_o):
    B, S, _ = h.shape
    M = m.shape[1]
    q = (h @ w_q).reshape(B, S, MEM_HEADS, MEM_HEAD_DIM)
    kv = m @ w_kv
    k = kv[..., :D_MODEL].reshape(B, M, MEM_HEADS, MEM_HEAD_DIM)
    v = kv[..., D_MODEL:].reshape(B, M, MEM_HEADS, MEM_HEAD_DIM)
    s = jnp.einsum('bshd,bmhd->bhsm', q, k).astype(jnp.float32) * (MEM_HEAD_DIM ** -0.5)
    p = jax.nn.softmax(s, axis=-1).astype(v.dtype)
    o = jnp.einsum('bhsm,bmhd->bshd', p, v).reshape(B, S, D_MODEL)
    return o @ w_o


def conv_gated_mlp(h, w_up, conv_w, w_down):
    up = h @ w_up
    g = causal_dwconv(up[..., :D_FF], conv_w)
    u = up[..., D_FF:]
    return (jax.nn.silu(g) * u) @ w_down


def setup_inputs(seed: int = 0) -> dict:
    key = jax.random.key(seed)
    ks = jax.random.split(key, 20)
    f32 = jnp.float32

    def w(k, shape, fan_in):
        return jax.random.normal(k, shape, f32) * (fan_in ** -0.5)

    def gain(k, shape):
        return 1.0 + 0.1 * jax.random.normal(k, shape, f32)

    return {
        "x": jax.random.normal(ks[0], (BATCH, SEQ, D_MODEL), f32),
        "mem": jax.random.normal(ks[1], (BATCH, MEM_LEN, D_MODEL), f32),
        "norm_mix_g": gain(ks[2], (DEPTH, D_MODEL)),
        "w_in_mix": w(ks[3], (DEPTH, D_MODEL, D_IN_MIX), D_MODEL),
        "conv_mix_w": w(ks[4], (DEPTH, CONV_WIDTH, D_CONV), CONV_WIDTH),
        "attn_out_g": gain(ks[5], (DEPTH, D_ATTN)),
        "conv_out_g": gain(ks[6], (DEPTH, D_CONV)),
        "w_out_mix": w(ks[7], (DEPTH, D_MIX, D_MODEL), D_MIX),
        "norm_mem_g": gain(ks[8], (DEPTH, D_MODEL)),
        "mem_kv_norm_g": gain(ks[9], (DEPTH, D_MODEL)),
        "w_mem_q": w(ks[10], (DEPTH, D_MODEL, D_MODEL), D_MODEL),
        "w_mem_kv": w(ks[11], (DEPTH, D_MODEL, 2 * D_MODEL), D_MODEL),
        "w_mem_o": w(ks[12], (DEPTH, D_MODEL, D_MODEL), D_MODEL),
        "norm_ffn_g": gain(ks[13], (DEPTH, D_MODEL)),
        "w_ffn_up": w(ks[14], (DEPTH, D_MODEL, 2 * D_FF), D_MODEL),
        "ffn_conv_w": w(ks[15], (DEPTH, FFN_CONV_WIDTH, D_FF), FFN_CONV_WIDTH),
        "w_ffn_down": w(ks[16], (DEPTH, D_FF, D_MODEL), D_FF),
        "final_norm_g": gain(ks[17], (D_MODEL,)),
    }


def reference(x, mem, norm_mix_g, w_in_mix, conv_mix_w, attn_out_g, conv_out_g,
              w_out_mix, norm_mem_g, mem_kv_norm_g, w_mem_q, w_mem_kv, w_mem_o,
              norm_ffn_g, w_ffn_up, ffn_conv_w, w_ffn_down, final_norm_g):
    for l in range(DEPTH):
        h = rmsnorm(x, norm_mix_g[l])
        x = x + parallel_mixer(h, w_in_mix[l], conv_mix_w[l], attn_out_g[l],
                               conv_out_g[l], w_out_mix[l])
        h = rmsnorm(x, norm_mem_g[l])
        m = rmsnorm(mem, mem_kv_norm_g[l])
        x = x + memory_cross_attention(h, m, w_mem_q[l], w_mem_kv[l], w_mem_o[l])
        h = rmsnorm(x, norm_ffn_g[l])
        x = x + conv_gated_mlp(h, w_ffn_up[l], ffn_conv_w[l], w_ffn_down[l])
    return rmsnorm(x, final_norm_g)
```

```python
import functools

import jax
import jax.numpy as jnp
from jax import lax
from jax.experimental import pallas as pl
from jax.experimental.pallas import tpu as pltpu

HEAD_DIM = 64
ATTN_HEADS = 8
D_ATTN = ATTN_HEADS * HEAD_DIM
MOBA_BLOCK = 256
MOBA_TOPK = 3
MEM_HEADS = 4
EPS = 1e-6

LANES = 128
SUBLANES = 8
VMEM_LIMIT_BYTES = 56 * 1024 * 1024

SEQ_TILE = MOBA_BLOCK
HEAD_PAIR = 2 * HEAD_DIM
MASKED = -1e30

_NT = (((1,), (1,)), ((), ()))


def _dot(a, b):
    return jnp.dot(a, b, preferred_element_type=jnp.float32)


def _dot_nt(a, b):
    return lax.dot_general(a, b, _NT, preferred_element_type=jnp.float32)


def _rms_scale(x):
    return lax.rsqrt(jnp.mean(x * x, axis=-1, keepdims=True) + EPS)


def _split_bf16(a):
    hi = a.astype(jnp.bfloat16)
    lo = (a - hi.astype(jnp.float32)).astype(jnp.bfloat16)
    return hi, lo


def _causal_conv3(z, tail, w):
    rows = lax.broadcasted_iota(jnp.int32, (SUBLANES, z.shape[1]), 0)
    z1 = pltpu.roll(z, 1, axis=0)
    z2 = pltpu.roll(z, 2, axis=0)
    t1 = pltpu.roll(tail, 1, axis=0)
    t2 = pltpu.roll(tail, 2, axis=0)
    z1 = jnp.concatenate([jnp.where(rows < 1, t1, z1[:SUBLANES]), z1[SUBLANES:]], axis=0)
    z2 = jnp.concatenate([jnp.where(rows < 2, t2, z2[:SUBLANES]), z2[SUBLANES:]], axis=0)
    return z2 * w[0:1] + z1 * w[1:2] + z * w[2:3]


def _mix_in_kernel(x_ref, g_ref, w_ref, cw_ref, cg_ref,
                   q_ref, k_ref, v_ref, yc_ref, sel_ref,
                   km_ref, tail_ref):
    s = pl.program_id(1)
    da = D_ATTN
    dc = cw_ref.shape[1]

    @pl.when(s == 0)
    def _():
        km_ref[...] = jnp.zeros_like(km_ref)
        tail_ref[...] = jnp.zeros_like(tail_ref)

    x = x_ref[0]
    h = ((x * _rms_scale(x)) * g_ref[...]).astype(jnp.bfloat16)

    q = _dot(h, w_ref[:, 0:da])
    k = _dot(h, w_ref[:, da:2 * da])
    v = _dot(h, w_ref[:, 2 * da:3 * da])
    q_ref[0] = (q * (HEAD_DIM ** -0.5)).astype(jnp.bfloat16)
    k_ref[0] = k.astype(jnp.bfloat16)
    v_ref[0] = v.astype(jnp.bfloat16)

    qh, ql = _split_bf16(q)
    kmh, kml = _split_bf16(km_ref[...])
    gate = _dot_nt(qh, kmh) + _dot_nt(qh, kml) + _dot_nt(ql, kmh)
    lane = lax.broadcasted_iota(jnp.int32, gate.shape, 1)
    blk = (lane >> 3) & (SUBLANES - 1)
    valid = blk < s
    gate = jnp.where(valid, gate, -jnp.inf)
    rank = jnp.zeros(gate.shape, jnp.int32)
    for step in range(1, SUBLANES):
        other = pltpu.roll(gate, ATTN_HEADS * step, axis=1)
        earlier = blk >= step
        beats = jnp.where(other > gate, 1, jnp.where((other == gate) & earlier, 1, 0))
        rank = rank + beats
    chosen = valid & (rank < MOBA_TOPK)
    sel_ref[0] = jnp.where(chosen, 0.0, MASKED).astype(jnp.float32)

    kmean = jnp.mean(k, axis=0, keepdims=True)
    row_head = lax.broadcasted_iota(jnp.int32, (SUBLANES, da), 0)
    col_head = lax.broadcasted_iota(jnp.int32, (SUBLANES, da), 1) // HEAD_DIM
    km_rows = jnp.where(row_head == col_head, kmean, 0.0)
    base = pl.multiple_of(s * SUBLANES, SUBLANES)
    km_ref[pl.ds(base, SUBLANES), :] = km_rows
    km_ref[pl.ds(base + km_ref.shape[0] // 2, SUBLANES), :] = km_rows

    o = 3 * da
    bg = _dot(h, w_ref[:, o:o + dc])
    cgate = _dot(h, w_ref[:, o + dc:o + 2 * dc])
    u = _dot(h, w_ref[:, o + 2 * dc:o + 3 * dc])
    z = cgate * u
    y = bg * _causal_conv3(z, tail_ref[...], cw_ref[...])
    tail_ref[...] = z[z.shape[0] - SUBLANES:]
    yc_ref[0] = ((y * _rms_scale(y)) * cg_ref[...]).astype(jnp.bfloat16)


def _mix_in(x, g, w_in, conv_w, conv_g):
    B, S, D = x.shape
    nb = S // SEQ_TILE
    dc = conv_w.shape[1]
    tile = lambda n: pl.BlockSpec((1, SEQ_TILE, n), lambda b, s: (b, s, 0))
    full = lambda a: pl.BlockSpec(a.shape, lambda b, s: (0,) * a.ndim)
    bf = jnp.bfloat16
    return pl.pallas_call(
        _mix_in_kernel,
        grid=(B, nb),
        in_specs=[tile(D), full(g), full(w_in), full(conv_w), full(conv_g)],
        out_specs=[tile(D_ATTN), tile(D_ATTN), tile(D_ATTN), tile(dc), tile(LANES)],
        out_shape=[jax.ShapeDtypeStruct((B, S, D_ATTN), bf)] * 3
        + [jax.ShapeDtypeStruct((B, S, dc), bf),
           jax.ShapeDtypeStruct((B, S, LANES), jnp.float32)],
        scratch_shapes=[pltpu.VMEM((LANES, D_ATTN), jnp.float32),
                        pltpu.VMEM((SUBLANES, dc), jnp.float32)],
        compiler_params=pltpu.CompilerParams(
            dimension_semantics=("parallel", "arbitrary"),
            vmem_limit_bytes=VMEM_LIMIT_BYTES),
        name="mix_in",
    )(x, g, w_in, conv_w, conv_g)


def _mix_attn_kernel(q_ref, k_ref, v_ref, sel_ref, yc_ref, x_ref, ag_ref, w_ref,
                     o_ref, qm_ref, m_ref, l_ref, acc_ref):
    j = pl.program_id(1)
    T = SEQ_TILE
    lane = lax.broadcasted_iota(jnp.int32, (T, HEAD_PAIR), 1)
    low_half = lane < HEAD_DIM

    for h in range(ATTN_HEADS):
        p = h // 2
        q2 = q_ref[0, :, p * HEAD_PAIR:(p + 1) * HEAD_PAIR]
        keep = low_half if h % 2 == 0 else jnp.logical_not(low_half)
        qm_ref[:, h * HEAD_PAIR:(h + 1) * HEAD_PAIR] = jnp.where(keep, q2, jnp.zeros_like(q2))

    own = pl.multiple_of(j * T, T)
    row = lax.broadcasted_iota(jnp.int32, (T, T), 0)
    col = lax.broadcasted_iota(jnp.int32, (T, T), 1)
    causal = col <= row
    for h in range(ATTN_HEADS):
        p = h // 2
        kp = k_ref[0, pl.ds(own, T), p * HEAD_PAIR:(p + 1) * HEAD_PAIR]
        vp = v_ref[0, pl.ds(own, T), p * HEAD_PAIR:(p + 1) * HEAD_PAIR]
        sc = _dot_nt(qm_ref[:, h * HEAD_PAIR:(h + 1) * HEAD_PAIR], kp)
        sc = jnp.where(causal, sc, MASKED)
        m = jnp.max(sc, axis=-1, keepdims=True)
        e = jnp.exp(sc - m)
        m_ref[h] = m
        l_ref[h] = jnp.sum(e, axis=-1, keepdims=True)
        acc_ref[h] = _dot(e.astype(jnp.bfloat16), vp)

    def past_block(n, carry):
        start = pl.multiple_of(n * T, T)
        shift = (LANES - n * ATTN_HEADS) % LANES
        bias = pltpu.roll(sel_ref[0], shift, axis=1)
        for h in range(ATTN_HEADS):
            p = h // 2
            kp = k_ref[0, pl.ds(start, T), p * HEAD_PAIR:(p + 1) * HEAD_PAIR]
            vp = v_ref[0, pl.ds(start, T), p * HEAD_PAIR:(p + 1) * HEAD_PAIR]
            sc = _dot_nt(qm_ref[:, h * HEAD_PAIR:(h + 1) * HEAD_PAIR], kp)
            sc = sc + bias[:, h:h + 1]
            m_old = m_ref[h]
            m_new = jnp.maximum(m_old, jnp.max(sc, axis=-1, keepdims=True))
            alpha = jnp.exp(m_old - m_new)
            e = jnp.exp(sc - m_new)
            m_ref[h] = m_new
            l_ref[h] = alpha * l_ref[h] + jnp.sum(e, axis=-1, keepdims=True)
            acc_ref[h] = alpha * acc_ref[h] + _dot(e.astype(jnp.bfloat16), vp)
        return carry

    lax.fori_loop(0, j, past_block, 0)

    outs = []
    for p in range(ATTN_HEADS // 2):
        o0 = acc_ref[2 * p] / l_ref[2 * p]
        o1 = acc_ref[2 * p + 1] / l_ref[2 * p + 1]
        outs.append(jnp.where(low_half, o0, o1))
    ya = jnp.concatenate(outs, axis=1)
    yan = ((ya * _rms_scale(ya)) * ag_ref[...]).astype(jnp.bfloat16)
    y = jnp.concatenate([yan, yc_ref[0]], axis=1)
    o_ref[0] = x_ref[0] + _dot(y, w_ref[...])


def _mix_attn(q, k, v, sel, yc, x, attn_g, w_out):
    B, S, D = x.shape
    nb = S // SEQ_TILE
    T = SEQ_TILE
    tile = lambda n: pl.BlockSpec((1, T, n), lambda b, s: (b, s, 0))
    seq = lambda n: pl.BlockSpec((1, S, n), lambda b, s: (b, 0, 0))
    full = lambda a: pl.BlockSpec(a.shape, lambda b, s: (0,) * a.ndim)
    return pl.pallas_call(
        _mix_attn_kernel,
        grid=(B, nb),
        in_specs=[tile(D_ATTN), seq(D_ATTN), seq(D_ATTN), tile(LANES),
                  tile(yc.shape[2]), tile(D), full(attn_g), full(w_out)],
        out_specs=tile(D),
        out_shape=jax.ShapeDtypeStruct((B, S, D), jnp.float32),
        scratch_shapes=[pltpu.VMEM((T, ATTN_HEADS * HEAD_PAIR), jnp.bfloat16),
                        pltpu.VMEM((ATTN_HEADS, T, 1), jnp.float32),
                        pltpu.VMEM((ATTN_HEADS, T, 1), jnp.float32),
                        pltpu.VMEM((ATTN_HEADS, T, HEAD_PAIR), jnp.float32)],
        compiler_params=pltpu.CompilerParams(
            dimension_semantics=("parallel", "arbitrary"),
            vmem_limit_bytes=VMEM_LIMIT_BYTES),
        name="mix_attn",
    )(q, k, v, sel, yc, x, attn_g, w_out)


def _mem_kv_kernel(m_ref, g_ref, w_ref, k_ref, v_ref):
    d = m_ref.shape[2]
    m = m_ref[0]
    h = ((m * _rms_scale(m)) * g_ref[...]).astype(jnp.bfloat16)
    k_ref[0] = _dot(h, w_ref[:, :d]).astype(jnp.bfloat16)
    v_ref[0] = _dot(h, w_ref[:, d:]).astype(jnp.bfloat16)


def _mem_kv(mem, g, w_kv):
    B, M, D = mem.shape
    blk = pl.BlockSpec((1, M, D), lambda b: (b, 0, 0))
    full = lambda a: pl.BlockSpec(a.shape, lambda b: (0,) * a.ndim)
    return pl.pallas_call(
        _mem_kv_kernel,
        grid=(B,),
        in_specs=[blk, full(g), full(w_kv)],
        out_specs=[blk, blk],
        out_shape=[jax.ShapeDtypeStruct((B, M, D), jnp.bfloat16)] * 2,
        compiler_params=pltpu.CompilerParams(
            dimension_semantics=("parallel",),
            vmem_limit_bytes=VMEM_LIMIT_BYTES),
        name="mem_kv",
    )(mem, g, w_kv)


def _mem_attn_kernel(x_ref, g_ref, wq_ref, k_ref, v_ref, wo_ref, o_ref):
    x = x_ref[0]
    d = x.shape[1]
    hd = d // MEM_HEADS
    h = ((x * _rms_scale(x)) * g_ref[...]).astype(jnp.bfloat16)
    q = (_dot(h, wq_ref[...]) * (hd ** -0.5)).astype(jnp.bfloat16)
    outs = []
    for i in range(MEM_HEADS):
        cols = slice(i * hd, (i + 1) * hd)
        sc = _dot_nt(q[:, cols], k_ref[0, :, cols])
        e = jnp.exp(sc - jnp.max(sc, axis=-1, keepdims=True))
        o = _dot(e.astype(jnp.bfloat16), v_ref[0, :, cols])
        outs.append((o / jnp.sum(e, axis=-1, keepdims=True)).astype(jnp.bfloat16))
    o_ref[0] = x + _dot(jnp.concatenate(outs, axis=1), wo_ref[...])


def _mem_attn(x, g, w_q, k_mem, v_mem, w_o):
    B, S, D = x.shape
    M = k_mem.shape[1]
    T = SEQ_TILE
    tile = pl.BlockSpec((1, T, D), lambda b, s: (b, s, 0))
    memb = pl.BlockSpec((1, M, D), lambda b, s: (b, 0, 0))
    full = lambda a: pl.BlockSpec(a.shape, lambda b, s: (0,) * a.ndim)
    return pl.pallas_call(
        _mem_attn_kernel,
        grid=(B, S // T),
        in_specs=[tile, full(g), full(w_q), memb, memb, full(w_o)],
        out_specs=tile,
        out_shape=jax.ShapeDtypeStruct((B, S, D), jnp.float32),
        compiler_params=pltpu.CompilerParams(
            dimension_semantics=("parallel", "parallel"),
            vmem_limit_bytes=VMEM_LIMIT_BYTES),
        name="mem_attn",
    )(x, g, w_q, k_mem, v_mem, w_o)


def _ffn_kernel(x_ref, g_ref, wu_ref, cw_ref, wd_ref, fg_ref, o_ref, tail_ref):
    s = pl.program_id(1)
    dff = cw_ref.shape[1]

    @pl.when(s == 0)
    def _():
        tail_ref[...] = jnp.zeros_like(tail_ref)

    x = x_ref[0]
    h = ((x * _rms_scale(x)) * g_ref[...]).astype(jnp.bfloat16)
    pre = _dot(h, wu_ref[:, :dff])
    gate = _causal_conv3(pre, tail_ref[...], cw_ref[...])
    tail_ref[...] = pre[pre.shape[0] - SUBLANES:]
    u = _dot(h, wu_ref[:, dff:])
    act = (gate * (1.0 / (1.0 + jnp.exp(-gate))) * u).astype(jnp.bfloat16)
    y = x + _dot(act, wd_ref[...])
    o_ref[0] = (y * _rms_scale(y)) * fg_ref[...]


def _ffn(x, g, w_up, conv_w, w_down, final_g):
    B, S, D = x.shape
    T = SEQ_TILE
    dff = conv_w.shape[1]
    tile = pl.BlockSpec((1, T, D), lambda b, s: (b, s, 0))
    full = lambda a: pl.BlockSpec(a.shape, lambda b, s: (0,) * a.ndim)
    return pl.pallas_call(
        _ffn_kernel,
        grid=(B, S // T),
        in_specs=[tile, full(g), full(w_up), full(conv_w), full(w_down), full(final_g)],
        out_specs=tile,
        out_shape=jax.ShapeDtypeStruct((B, S, D), jnp.float32),
        scratch_shapes=[pltpu.VMEM((SUBLANES, dff), jnp.float32)],
        compiler_params=pltpu.CompilerParams(
            dimension_semantics=("parallel", "arbitrary"),
            vmem_limit_bytes=VMEM_LIMIT_BYTES),
        name="ffn",
    )(x, g, w_up, conv_w, w_down, final_g)


def kernel(x, mem, norm_mix_g, w_in_mix, conv_mix_w, attn_out_g, conv_out_g, w_out_mix, norm_mem_g, mem_kv_norm_g, w_mem_q, w_mem_kv, w_mem_o, norm_ffn_g, w_ffn_up, ffn_conv_w, w_ffn_down, final_norm_g):
    depth = w_in_mix.shape[0]
    bf = jnp.bfloat16
    row = lambda a: a.reshape(1, -1)
    for l in range(depth):
        q, k, v, yc, sel = _mix_in(x, row(norm_mix_g[l]), w_in_mix[l].astype(bf),
                                   conv_mix_w[l], row(conv_out_g[l]))
        x = _mix_attn(q, k, v, sel, yc, x, row(attn_out_g[l]), w_out_mix[l].astype(bf))
        k_mem, v_mem = _mem_kv(mem, row(mem_kv_norm_g[l]), w_mem_kv[l].astype(bf))
        x = _mem_attn(x, row(norm_mem_g[l]), w_mem_q[l].astype(bf), k_mem, v_mem,
                      w_mem_o[l].astype(bf))
        assert depth == 1
        x = _ffn(x, row(norm_ffn_g[l]), w_ffn_up[l].astype(bf), ffn_conv_w[l],
                 w_ffn_down[l].astype(bf), row(final_norm_g))
    return x
```

```python
import jax
import jax.numpy as jnp
from jax import lax
from jax.experimental import pallas as pl
from jax.experimental.pallas import tpu as pltpu

HEAD_DIM = 64
ATTN_HEADS = 8
D_ATTN = ATTN_HEADS * HEAD_DIM
MOBA_BLOCK = 256
MOBA_TOPK = 3
MEM_HEADS = 4
EPS = 1e-6

LANES = 128
SUBLANES = 8
VMEM_LIMIT_BYTES = 56 * 1024 * 1024

SEQ_TILE = MOBA_BLOCK
HEAD_PAIR = 2 * HEAD_DIM
MASKED = -1e30
LOG2E = 1.4426950408889634

_NT = (((1,), (1,)), ((), ()))


def _dot(a, b):
    return jnp.dot(a, b, preferred_element_type=jnp.float32)


def _dot_nt(a, b):
    return lax.dot_general(a, b, _NT, preferred_element_type=jnp.float32)


def _rms_scale(x):
    return lax.rsqrt(jnp.mean(x * x, axis=-1, keepdims=True) + EPS)


def _split_bf16(a):
    hi = a.astype(jnp.bfloat16)
    lo = (a - hi.astype(jnp.float32)).astype(jnp.bfloat16)
    return hi, lo


def _causal_conv3(z, tail, w):
    rows = lax.broadcasted_iota(jnp.int32, (SUBLANES, z.shape[1]), 0)
    z1 = pltpu.roll(z, 1, axis=0)
    z2 = pltpu.roll(z, 2, axis=0)
    t1 = pltpu.roll(tail, 1, axis=0)
    t2 = pltpu.roll(tail, 2, axis=0)
    z1 = jnp.concatenate([jnp.where(rows < 1, t1, z1[:SUBLANES]), z1[SUBLANES:]], axis=0)
    z2 = jnp.concatenate([jnp.where(rows < 2, t2, z2[:SUBLANES]), z2[SUBLANES:]], axis=0)
    return z2 * w[0:1] + z1 * w[1:2] + z * w[2:3]


def _mix_in_kernel(x_ref, g_ref, w_ref, wvt_ref, cw_ref, cg_ref,
                   q_ref, k_ref, vt_ref, yc_ref, sel_ref,
                   km_ref, tail_ref):
    s = pl.program_id(1)
    da = D_ATTN
    dc = cw_ref.shape[1]
    nblk = km_ref.shape[0] // ATTN_HEADS

    @pl.when(s == 0)
    def _():
        km_ref[...] = jnp.zeros_like(km_ref)
        tail_ref[...] = jnp.zeros_like(tail_ref)

    x = x_ref[0]
    h = ((x * _rms_scale(x)) * g_ref[...]).astype(jnp.bfloat16)

    q = _dot(h, w_ref[:, 0:da])
    k = _dot(h, w_ref[:, da:2 * da])
    q_ref[0] = (q * (HEAD_DIM ** -0.5 * LOG2E)).astype(jnp.bfloat16)
    k_ref[0] = k.astype(jnp.bfloat16)
    vt_ref[0, 0] = _dot_nt(wvt_ref[...], h).astype(jnp.bfloat16)

    qh, ql = _split_bf16(q)
    kmh, kml = _split_bf16(km_ref[...])
    gate = _dot_nt(kmh, qh) + _dot_nt(kml, qh) + _dot_nt(kmh, ql)
    g = [jnp.where(n < s, gate[n * ATTN_HEADS:(n + 1) * ATTN_HEADS], -jnp.inf)
         for n in range(nblk)]
    bias = []
    for n in range(nblk):
        rank = jnp.zeros(g[n].shape, jnp.int32)
        for o in range(nblk):
            if o != n:
                beats = (g[o] >= g[n]) if o < n else (g[o] > g[n])
                rank = rank + jnp.where(beats, 1, 0)
        chosen = jnp.logical_and(n < s, rank < MOBA_TOPK)
        bias.append(jnp.where(chosen, 0.0, MASKED).astype(jnp.float32))
    sel_ref[0] = jnp.concatenate(bias, axis=0)

    kmean = jnp.mean(k, axis=0, keepdims=True)
    row_head = lax.broadcasted_iota(jnp.int32, (ATTN_HEADS, da), 0)
    col_head = lax.broadcasted_iota(jnp.int32, (ATTN_HEADS, da), 1) // HEAD_DIM
    base = pl.multiple_of(s * ATTN_HEADS, ATTN_HEADS)
    km_ref[pl.ds(base, ATTN_HEADS), :] = jnp.where(row_head == col_head, kmean, 0.0)

    o = 3 * da
    bg = _dot(h, w_ref[:, o:o + dc])
    cgate = _dot(h, w_ref[:, o + dc:o + 2 * dc])
    u = _dot(h, w_ref[:, o + 2 * dc:o + 3 * dc])
    z = cgate * u
    y = bg * _causal_conv3(z, tail_ref[...], cw_ref[...])
    tail_ref[...] = z[z.shape[0] - SUBLANES:]
    yc_ref[0] = ((y * _rms_scale(y)) * cg_ref[...]).astype(jnp.bfloat16)


def _mix_in(x, g, w_in, w_vt, conv_w, conv_g):
    B, S, D = x.shape
    T = SEQ_TILE
    nb = S // T
    dc = conv_w.shape[1]
    tile = lambda n: pl.BlockSpec((1, T, n), lambda b, s: (b, s, 0))
    full = lambda a: pl.BlockSpec(a.shape, lambda b, s: (0,) * a.ndim)
    bf = jnp.bfloat16
    return pl.pallas_call(
        _mix_in_kernel,
        grid=(B, nb),
        in_specs=[tile(D), full(g), full(w_in), full(w_vt), full(conv_w), full(conv_g)],
        out_specs=[tile(D_ATTN), tile(D_ATTN),
                   pl.BlockSpec((1, 1, D_ATTN, T), lambda b, s: (b, s, 0, 0)),
                   tile(dc),
                   pl.BlockSpec((1, nb * ATTN_HEADS, T), lambda b, s: (b, 0, s))],
        out_shape=[jax.ShapeDtypeStruct((B, S, D_ATTN), bf),
                   jax.ShapeDtypeStruct((B, S, D_ATTN), bf),
                   jax.ShapeDtypeStruct((B, nb, D_ATTN, T), bf),
                   jax.ShapeDtypeStruct((B, S, dc), bf),
                   jax.ShapeDtypeStruct((B, nb * ATTN_HEADS, S), jnp.float32)],
        scratch_shapes=[pltpu.VMEM((nb * ATTN_HEADS, D_ATTN), jnp.float32),
                        pltpu.VMEM((SUBLANES, dc), jnp.float32)],
        compiler_params=pltpu.CompilerParams(
            dimension_semantics=("parallel", "arbitrary"),
            vmem_limit_bytes=VMEM_LIMIT_BYTES),
        name="mix_in",
    )(x, g, w_in, w_vt, conv_w, conv_g)


def _mix_attn_kernel(q_ref, k_ref, vt_ref, sel_ref, yc_ref, x_ref, ag_ref, w_ref,
                     o_ref, qm_ref, m_ref, l_ref, acc_ref):
    j = pl.program_id(1)
    T = SEQ_TILE
    lane = lax.broadcasted_iota(jnp.int32, (T, HEAD_PAIR), 1)
    low_half = lane < HEAD_DIM

    for h in range(ATTN_HEADS):
        p = h // 2
        q2 = q_ref[0, :, p * HEAD_PAIR:(p + 1) * HEAD_PAIR]
        keep = low_half if h % 2 == 0 else jnp.logical_not(low_half)
        qm_ref[:, h * HEAD_PAIR:(h + 1) * HEAD_PAIR] = jnp.where(keep, q2, jnp.zeros_like(q2))

    def attend(start, blk, bias_row0):
        def scores(h):
            p = h // 2
            kp = k_ref[0, pl.ds(start, T), p * HEAD_PAIR:(p + 1) * HEAD_PAIR]
            sc = _dot_nt(kp, qm_ref[:, h * HEAD_PAIR:(h + 1) * HEAD_PAIR])
            if bias_row0 is None:
                sc = jnp.where(causal, sc, MASKED)
            return sc

        sc_next = scores(0)
        for h in range(ATTN_HEADS):
            sc = sc_next
            if h + 1 < ATTN_HEADS:
                sc_next = scores(h + 1)
            rows = slice(h * HEAD_DIM, (h + 1) * HEAD_DIM)
            sc3 = sc.reshape(T // SUBLANES, SUBLANES, T)
            mx = jnp.max(sc3, axis=0)
            for sh in (4, 2, 1):
                mx = jnp.maximum(mx, pltpu.roll(mx, sh, axis=0))
            if bias_row0 is None:
                m_new = mx
                shift = mx
            else:
                bias = jnp.broadcast_to(sel_ref[0, pl.ds(bias_row0 + h, 1), :], (SUBLANES, T))
                m_old = m_ref[h]
                m_new = jnp.maximum(m_old, mx + bias)
                alpha = jnp.exp2(m_old - m_new)
                shift = m_new - bias
            e3 = jnp.exp2(sc3 - shift[None])
            lsum = jnp.sum(e3, axis=0)
            pv = _dot(vt_ref[0, blk, rows, :], e3.reshape(T, T).astype(jnp.bfloat16))
            m_ref[h] = m_new
            if bias_row0 is None:
                l_ref[h] = lsum
                acc_ref[rows, :] = pv
            else:
                l_ref[h] = alpha * l_ref[h] + lsum
                acc3 = acc_ref[rows, :].reshape(HEAD_DIM // SUBLANES, SUBLANES, T)
                acc3 = acc3 * alpha[None] + pv.reshape(HEAD_DIM // SUBLANES, SUBLANES, T)
                acc_ref[rows, :] = acc3.reshape(HEAD_DIM, T)

    key = lax.broadcasted_iota(jnp.int32, (T, T), 0)
    qry = lax.broadcasted_iota(jnp.int32, (T, T), 1)
    causal = key <= qry
    attend(pl.multiple_of(j * T, T), j, None)

    def past_block(n, carry):
        attend(pl.multiple_of(n * T, T), n, n * ATTN_HEADS)
        return carry

    lax.fori_loop(0, j, past_block, 0)

    yat = jnp.concatenate(
        [acc_ref[h * HEAD_DIM:(h + 1) * HEAD_DIM, :] / jnp.sum(l_ref[h], axis=0, keepdims=True)
         for h in range(ATTN_HEADS)], axis=0)
    ya = yat.T
    yan = ((ya * _rms_scale(ya)) * ag_ref[...]).astype(jnp.bfloat16)
    y = jnp.concatenate([yan, yc_ref[0]], axis=1)
    o_ref[0] = x_ref[0] + _dot(y, w_ref[...])


def _mix_attn(q, k, vt, sel, yc, x, attn_g, w_out):
    B, S, D = x.shape
    T = SEQ_TILE
    nb = S // T
    tile = lambda n: pl.BlockSpec((1, T, n), lambda b, s: (b, s, 0))
    seq = lambda n: pl.BlockSpec((1, S, n), lambda b, s: (b, 0, 0))
    full = lambda a: pl.BlockSpec(a.shape, lambda b, s: (0,) * a.ndim)
    return pl.pallas_call(
        _mix_attn_kernel,
        grid=(B, nb),
        in_specs=[tile(D_ATTN), seq(D_ATTN),
                  pl.BlockSpec((1, nb, D_ATTN, T), lambda b, s: (b, 0, 0, 0)),
                  pl.BlockSpec((1, nb * ATTN_HEADS, T), lambda b, s: (b, 0, s)),
                  tile(yc.shape[2]), tile(D), full(attn_g), full(w_out)],
        out_specs=tile(D),
        out_shape=jax.ShapeDtypeStruct((B, S, D), jnp.float32),
        scratch_shapes=[pltpu.VMEM((T, ATTN_HEADS * HEAD_PAIR), jnp.bfloat16),
                        pltpu.VMEM((ATTN_HEADS, SUBLANES, T), jnp.float32),
                        pltpu.VMEM((ATTN_HEADS, SUBLANES, T), jnp.float32),
                        pltpu.VMEM((D_ATTN, T), jnp.float32)],
        compiler_params=pltpu.CompilerParams(
            dimension_semantics=("parallel", "arbitrary"),
            vmem_limit_bytes=VMEM_LIMIT_BYTES),
        name="mix_attn",
    )(q, k, vt, sel, yc, x, attn_g, w_out)


def _mem_kv_kernel(m_ref, g_ref, w_ref, k_ref, v_ref):
    d = m_ref.shape[2]
    m = m_ref[0]
    h = ((m * _rms_scale(m)) * g_ref[...]).astype(jnp.bfloat16)
    k_ref[0] = _dot(h, w_ref[:, :d]).astype(jnp.bfloat16)
    v_ref[0] = _dot(h, w_ref[:, d:]).astype(jnp.bfloat16)


def _mem_kv(mem, g, w_kv):
    B, M, D = mem.shape
    blk = pl.BlockSpec((1, M, D), lambda b: (b, 0, 0))
    full = lambda a: pl.BlockSpec(a.shape, lambda b: (0,) * a.ndim)
    return pl.pallas_call(
        _mem_kv_kernel,
        grid=(B,),
        in_specs=[blk, full(g), full(w_kv)],
        out_specs=[blk, blk],
        out_shape=[jax.ShapeDtypeStruct((B, M, D), jnp.bfloat16)] * 2,
        compiler_params=pltpu.CompilerParams(
            dimension_semantics=("parallel",),
            vmem_limit_bytes=VMEM_LIMIT_BYTES),
        name="mem_kv",
    )(mem, g, w_kv)


def _mem_attn_kernel(x_ref, g_ref, wq_ref, k_ref, v_ref, wo_ref, o_ref):
    x = x_ref[0]
    d = x.shape[1]
    hd = d // MEM_HEADS
    h = ((x * _rms_scale(x)) * g_ref[...]).astype(jnp.bfloat16)
    q = (_dot(h, wq_ref[...]) * (hd ** -0.5)).astype(jnp.bfloat16)
    outs = []
    for i in range(MEM_HEADS):
        cols = slice(i * hd, (i + 1) * hd)
        sc = _dot_nt(q[:, cols], k_ref[0, :, cols])
        e = jnp.exp(sc - jnp.max(sc, axis=-1, keepdims=True))
        o = _dot(e.astype(jnp.bfloat16), v_ref[0, :, cols])
        outs.append((o / jnp.sum(e, axis=-1, keepdims=True)).astype(jnp.bfloat16))
    o_ref[0] = x + _dot(jnp.concatenate(outs, axis=1), wo_ref[...])


def _mem_attn(x, g, w_q, k_mem, v_mem, w_o):
    B, S, D = x.shape
    M = k_mem.shape[1]
    T = SEQ_TILE
    tile = pl.BlockSpec((1, T, D), lambda b, s: (b, s, 0))
    memb = pl.BlockSpec((1, M, D), lambda b, s: (b, 0, 0))
    full = lambda a: pl.BlockSpec(a.shape, lambda b, s: (0,) * a.ndim)
    return pl.pallas_call(
        _mem_attn_kernel,
        grid=(B, S // T),
        in_specs=[tile, full(g), full(w_q), memb, memb, full(w_o)],
        out_specs=tile,
        out_shape=jax.ShapeDtypeStruct((B, S, D), jnp.float32),
        compiler_params=pltpu.CompilerParams(
            dimension_semantics=("parallel", "parallel"),
            vmem_limit_bytes=VMEM_LIMIT_BYTES),
        name="mem_attn",
    )(x, g, w_q, k_mem, v_mem, w_o)


def _ffn_kernel(x_ref, g_ref, wu_ref, cw_ref, wd_ref, fg_ref, o_ref, tail_ref):
    s = pl.program_id(1)
    dff = cw_ref.shape[1]

    @pl.when(s == 0)
    def _():
        tail_ref[...] = jnp.zeros_like(tail_ref)

    x = x_ref[0]
    h = ((x * _rms_scale(x)) * g_ref[...]).astype(jnp.bfloat16)
    pre = _dot(h, wu_ref[:, :dff])
    gate = _causal_conv3(pre, tail_ref[...], cw_ref[...])
    tail_ref[...] = pre[pre.shape[0] - SUBLANES:]
    u = _dot(h, wu_ref[:, dff:])
    act = (gate * (1.0 / (1.0 + jnp.exp(-gate))) * u).astype(jnp.bfloat16)
    y = x + _dot(act, wd_ref[...])
    o_ref[0] = (y * _rms_scale(y)) * fg_ref[...]


def _ffn(x, g, w_up, conv_w, w_down, final_g):
    B, S, D = x.shape
    T = SEQ_TILE
    dff = conv_w.shape[1]
    tile = pl.BlockSpec((1, T, D), lambda b, s: (b, s, 0))
    full = lambda a: pl.BlockSpec(a.shape, lambda b, s: (0,) * a.ndim)
    return pl.pallas_call(
        _ffn_kernel,
        grid=(B, S // T),
        in_specs=[tile, full(g), full(w_up), full(conv_w), full(w_down), full(final_g)],
        out_specs=tile,
        out_shape=jax.ShapeDtypeStruct((B, S, D), jnp.float32),
        scratch_shapes=[pltpu.VMEM((SUBLANES, dff), jnp.float32)],
        compiler_params=pltpu.CompilerParams(
            dimension_semantics=("parallel", "arbitrary"),
            vmem_limit_bytes=VMEM_LIMIT_BYTES),
        name="ffn",
    )(x, g, w_up, conv_w, w_down, final_g)


def kernel(x, mem, norm_mix_g, w_in_mix, conv_mix_w, attn_out_g, conv_out_g, w_out_mix, norm_mem_g, mem_kv_norm_g, w_mem_q, w_mem_kv, w_mem_o, norm_ffn_g, w_ffn_up, ffn_conv_w, w_ffn_down, final_norm_g):
    depth = w_in_mix.shape[0]
    bf = jnp.bfloat16
    row = lambda a: a.reshape(1, -1)
    for l in range(depth):
        w_in = w_in_mix[l].astype(bf)
        w_vt = w_in[:, 2 * D_ATTN:3 * D_ATTN].T
        q, k, vt, yc, sel = _mix_in(x, row(norm_mix_g[l]), w_in, w_vt,
                                    conv_mix_w[l], row(conv_out_g[l]))
        x = _mix_attn(q, k, vt, sel, yc, x, row(attn_out_g[l]), w_out_mix[l].astype(bf))
        k_mem, v_mem = _mem_kv(mem, row(mem_kv_norm_g[l]), w_mem_kv[l].astype(bf))
        x = _mem_attn(x, row(norm_mem_g[l]), w_mem_q[l].astype(bf), k_mem, v_mem,
                      w_mem_o[l].astype(bf))
        assert depth == 1
        x = _ffn(x, row(norm_ffn_g[l]), w_ffn_up[l].astype(bf), ffn_conv_w[l],
                 w_ffn_down[l].astype(bf), row(final_norm_g))
    return x
```

```python
import jax
import jax.numpy as jnp
from jax import lax
from jax.experimental import pallas as pl
from jax.experimental.pallas import tpu as pltpu

HEAD_DIM = 64
ATTN_HEADS = 8
D_ATTN = ATTN_HEADS * HEAD_DIM
MOBA_BLOCK = 256
MOBA_TOPK = 3
MEM_HEADS = 4
EPS = 1e-6

LANES = 128
SUBLANES = 8
VMEM_LIMIT_BYTES = 56 * 1024 * 1024

SEQ_TILE = MOBA_BLOCK
HEAD_PAIR = 2 * HEAD_DIM
MASKED = -1e30
LOG2E = 1.4426950408889634

_NT = (((1,), (1,)), ((), ()))


def _dot(a, b):
    return jnp.dot(a, b, preferred_element_type=jnp.float32)


def _dot_nt(a, b):
    return lax.dot_general(a, b, _NT, preferred_element_type=jnp.float32)


def _rms_scale(x):
    return lax.rsqrt(jnp.mean(x * x, axis=-1, keepdims=True) + EPS)


def _split_bf16(a):
    hi = a.astype(jnp.bfloat16)
    lo = (a - hi.astype(jnp.float32)).astype(jnp.bfloat16)
    return hi, lo


def _causal_conv3(z, tail, w):
    rows = lax.broadcasted_iota(jnp.int32, (SUBLANES, z.shape[1]), 0)
    z1 = pltpu.roll(z, 1, axis=0)
    z2 = pltpu.roll(z, 2, axis=0)
    t1 = pltpu.roll(tail, 1, axis=0)
    t2 = pltpu.roll(tail, 2, axis=0)
    z1 = jnp.concatenate([jnp.where(rows < 1, t1, z1[:SUBLANES]), z1[SUBLANES:]], axis=0)
    z2 = jnp.concatenate([jnp.where(rows < 2, t2, z2[:SUBLANES]), z2[SUBLANES:]], axis=0)
    return z2 * w[0:1] + z1 * w[1:2] + z * w[2:3]


def _mix_in_kernel(x_ref, g_ref, w_ref, wvt_ref, cw_ref, cg_ref,
                   q_ref, k_ref, vt_ref, yc_ref, sel_ref,
                   km_ref, tail_ref):
    s = pl.program_id(1)
    da = D_ATTN
    dc = cw_ref.shape[1]
    nblk = km_ref.shape[0] // ATTN_HEADS

    @pl.when(s == 0)
    def _():
        km_ref[...] = jnp.zeros_like(km_ref)
        tail_ref[...] = jnp.zeros_like(tail_ref)

    x = x_ref[0]
    h = ((x * _rms_scale(x)) * g_ref[...]).astype(jnp.bfloat16)

    q = _dot(h, w_ref[:, 0:da])
    k = _dot(h, w_ref[:, da:2 * da])
    q_ref[0] = (q * (HEAD_DIM ** -0.5 * LOG2E)).astype(jnp.bfloat16)
    k_ref[0] = k.astype(jnp.bfloat16)
    vt_ref[0, 0] = _dot_nt(wvt_ref[...], h).astype(jnp.bfloat16)

    qh, ql = _split_bf16(q)
    kmh, kml = _split_bf16(km_ref[...])
    gate = _dot_nt(kmh, qh) + _dot_nt(kml, qh) + _dot_nt(kmh, ql)
    g = [jnp.where(n < s, gate[n * ATTN_HEADS:(n + 1) * ATTN_HEADS], -jnp.inf)
         for n in range(nblk)]
    bias = []
    for n in range(nblk):
        rank = jnp.zeros(g[n].shape, jnp.int32)
        for o in range(nblk):
            if o != n:
                beats = (g[o] >= g[n]) if o < n else (g[o] > g[n])
                rank = rank + jnp.where(beats, 1, 0)
        chosen = jnp.logical_and(n < s, rank < MOBA_TOPK)
        bias.append(jnp.where(chosen, 0.0, MASKED).astype(jnp.float32))
    sel_ref[0] = jnp.concatenate(bias, axis=0)

    kmean = jnp.mean(k, axis=0, keepdims=True)
    row_head = lax.broadcasted_iota(jnp.int32, (ATTN_HEADS, da), 0)
    col_head = lax.broadcasted_iota(jnp.int32, (ATTN_HEADS, da), 1) // HEAD_DIM
    base = pl.multiple_of(s * ATTN_HEADS, ATTN_HEADS)
    km_ref[pl.ds(base, ATTN_HEADS), :] = jnp.where(row_head == col_head, kmean, 0.0)

    o = 3 * da
    bg = _dot(h, w_ref[:, o:o + dc])
    cgate = _dot(h, w_ref[:, o + dc:o + 2 * dc])
    u = _dot(h, w_ref[:, o + 2 * dc:o + 3 * dc])
    z = cgate * u
    y = bg * _causal_conv3(z, tail_ref[...], cw_ref[...])
    tail_ref[...] = z[z.shape[0] - SUBLANES:]
    yc_ref[0] = ((y * _rms_scale(y)) * cg_ref[...]).astype(jnp.bfloat16)


def _mix_in(x, g, w_in, w_vt, conv_w, conv_g):
    B, S, D = x.shape
    T = SEQ_TILE
    nb = S // T
    dc = conv_w.shape[1]
    tile = lambda n: pl.BlockSpec((1, T, n), lambda b, s: (b, s, 0))
    full = lambda a: pl.BlockSpec(a.shape, lambda b, s: (0,) * a.ndim)
    bf = jnp.bfloat16
    return pl.pallas_call(
        _mix_in_kernel,
        grid=(B, nb),
        in_specs=[tile(D), full(g), full(w_in), full(w_vt), full(conv_w), full(conv_g)],
        out_specs=[tile(D_ATTN), tile(D_ATTN),
                   pl.BlockSpec((1, 1, D_ATTN, T), lambda b, s: (b, s, 0, 0)),
                   tile(dc),
                   pl.BlockSpec((1, nb * ATTN_HEADS, T), lambda b, s: (b, 0, s))],
        out_shape=[jax.ShapeDtypeStruct((B, S, D_ATTN), bf),
                   jax.ShapeDtypeStruct((B, S, D_ATTN), bf),
                   jax.ShapeDtypeStruct((B, nb, D_ATTN, T), bf),
                   jax.ShapeDtypeStruct((B, S, dc), bf),
                   jax.ShapeDtypeStruct((B, nb * ATTN_HEADS, S), jnp.float32)],
        scratch_shapes=[pltpu.VMEM((nb * ATTN_HEADS, D_ATTN), jnp.float32),
                        pltpu.VMEM((SUBLANES, dc), jnp.float32)],
        compiler_params=pltpu.CompilerParams(
            dimension_semantics=("parallel", "arbitrary"),
            vmem_limit_bytes=VMEM_LIMIT_BYTES),
        name="mix_in",
    )(x, g, w_in, w_vt, conv_w, conv_g)


def _mix_attn_kernel(q_ref, k_ref, vt_ref, sel_ref, yc_ref, x_ref, ag_ref, w_ref,
                     o_ref, qm_ref, s0_ref, mx0_ref, s1_ref, mx1_ref, m_ref, l_ref, acc_ref):
    j = pl.program_id(1)
    T = SEQ_TILE
    nblk = vt_ref.shape[1]
    lane = lax.broadcasted_iota(jnp.int32, (T, HEAD_PAIR), 1)
    low_half = lane < HEAD_DIM

    for h in range(ATTN_HEADS):
        p = h // 2
        q2 = q_ref[0, :, p * HEAD_PAIR:(p + 1) * HEAD_PAIR]
        keep = low_half if h % 2 == 0 else jnp.logical_not(low_half)
        qm_ref[:, h * HEAD_PAIR:(h + 1) * HEAD_PAIR] = jnp.where(keep, q2, jnp.zeros_like(q2))

    key = lax.broadcasted_iota(jnp.int32, (T, T), 0)
    qry = lax.broadcasted_iota(jnp.int32, (T, T), 1)
    causal = key <= qry

    def score_head(h, blk, s_ref, mx_ref, own):
        p = h // 2
        start = pl.multiple_of(blk * T, T)
        kp = k_ref[0, pl.ds(start, T), p * HEAD_PAIR:(p + 1) * HEAD_PAIR]
        sc = _dot_nt(kp, qm_ref[:, h * HEAD_PAIR:(h + 1) * HEAD_PAIR])
        if own:
            sc = jnp.where(causal, sc, MASKED)
        s_ref[h] = sc
        mx = jnp.max(sc.reshape(T // SUBLANES, SUBLANES, T), axis=0)
        for sh in (4, 2, 1):
            mx = jnp.maximum(mx, pltpu.roll(mx, sh, axis=0))
        mx_ref[h] = mx

    def softmax_head(h, blk, s_ref, mx_ref, own):
        rows = slice(h * HEAD_DIM, (h + 1) * HEAD_DIM)
        mx = mx_ref[h]
        if own:
            m_new = mx
            shift = mx
        else:
            bias = jnp.broadcast_to(sel_ref[0, pl.ds(blk * ATTN_HEADS + h, 1), :], (SUBLANES, T))
            m_old = m_ref[h]
            m_new = jnp.maximum(m_old, mx + bias)
            alpha = jnp.exp2(m_old - m_new)
            shift = m_new - bias
        e3 = jnp.exp2(s_ref[h].reshape(T // SUBLANES, SUBLANES, T) - shift[None])
        lsum = jnp.sum(e3, axis=0)
        pv = _dot(vt_ref[0, blk, rows, :], e3.reshape(T, T).astype(jnp.bfloat16))
        m_ref[h] = m_new
        if own:
            l_ref[h] = lsum
            acc_ref[rows, :] = pv
        else:
            l_ref[h] = alpha * l_ref[h] + lsum
            acc3 = acc_ref[rows, :].reshape(HEAD_DIM // SUBLANES, SUBLANES, T)
            acc3 = acc3 * alpha[None] + pv.reshape(HEAD_DIM // SUBLANES, SUBLANES, T)
            acc_ref[rows, :] = acc3.reshape(HEAD_DIM, T)

    def stage(cur, blk, own, nxt, nxt_blk):
        for h in range(ATTN_HEADS):
            softmax_head(h, blk, *cur, own)
            if nxt is not None:
                score_head(h, nxt_blk, *nxt, False)

    buf0 = (s0_ref, mx0_ref)
    buf1 = (s1_ref, mx1_ref)

    for h in range(ATTN_HEADS):
        score_head(h, j, *buf0, True)
    stage(buf0, j, True, buf1, 0)

    def past_pair(i, carry):
        n = 2 * i
        stage(buf1, n, False, buf0, n + 1)
        stage(buf0, n + 1, False, buf1, jnp.minimum(n + 2, nblk - 1))
        return carry

    lax.fori_loop(0, j // 2, past_pair, 0)

    @pl.when(j % 2 == 1)
    def _():
        stage(buf1, j - 1, False, None, None)

    yat = jnp.concatenate(
        [acc_ref[h * HEAD_DIM:(h + 1) * HEAD_DIM, :] / jnp.sum(l_ref[h], axis=0, keepdims=True)
         for h in range(ATTN_HEADS)], axis=0)
    ya = yat.T
    yan = ((ya * _rms_scale(ya)) * ag_ref[...]).astype(jnp.bfloat16)
    y = jnp.concatenate([yan, yc_ref[0]], axis=1)
    o_ref[0] = x_ref[0] + _dot(y, w_ref[...])


def _mix_attn(q, k, vt, sel, yc, x, attn_g, w_out):
    B, S, D = x.shape
    T = SEQ_TILE
    nb = S // T
    tile = lambda n: pl.BlockSpec((1, T, n), lambda b, s: (b, s, 0))
    seq = lambda n: pl.BlockSpec((1, S, n), lambda b, s: (b, 0, 0))
    full = lambda a: pl.BlockSpec(a.shape, lambda b, s: (0,) * a.ndim)
    stat = pltpu.VMEM((ATTN_HEADS, SUBLANES, T), jnp.float32)
    return pl.pallas_call(
        _mix_attn_kernel,
        grid=(B, nb),
        in_specs=[tile(D_ATTN), seq(D_ATTN),
                  pl.BlockSpec((1, nb, D_ATTN, T), lambda b, s: (b, 0, 0, 0)),
                  pl.BlockSpec((1, nb * ATTN_HEADS, T), lambda b, s: (b, 0, s)),
                  tile(yc.shape[2]), tile(D), full(attn_g), full(w_out)],
        out_specs=tile(D),
        out_shape=jax.ShapeDtypeStruct((B, S, D), jnp.float32),
        scratch_shapes=[pltpu.VMEM((T, ATTN_HEADS * HEAD_PAIR), jnp.bfloat16),
                        pltpu.VMEM((ATTN_HEADS, T, T), jnp.float32),
                        stat,
                        pltpu.VMEM((ATTN_HEADS, T, T), jnp.float32),
                        stat, stat, stat,
                        pltpu.VMEM((D_ATTN, T), jnp.float32)],
        compiler_params=pltpu.CompilerParams(
            dimension_semantics=("parallel", "arbitrary"),
            vmem_limit_bytes=VMEM_LIMIT_BYTES),
        name="mix_attn",
    )(q, k, vt, sel, yc, x, attn_g, w_out)


def _mem_kv_kernel(m_ref, g_ref, w_ref, k_ref, v_ref):
    d = m_ref.shape[2]
    m = m_ref[0]
    h = ((m * _rms_scale(m)) * g_ref[...]).astype(jnp.bfloat16)
    k_ref[0] = _dot(h, w_ref[:, :d]).astype(jnp.bfloat16)
    v_ref[0] = _dot(h, w_ref[:, d:]).astype(jnp.bfloat16)


def _mem_kv(mem, g, w_kv):
    B, M, D = mem.shape
    blk = pl.BlockSpec((1, M, D), lambda b: (b, 0, 0))
    full = lambda a: pl.BlockSpec(a.shape, lambda b: (0,) * a.ndim)
    return pl.pallas_call(
        _mem_kv_kernel,
        grid=(B,),
        in_specs=[blk, full(g), full(w_kv)],
        out_specs=[blk, blk],
        out_shape=[jax.ShapeDtypeStruct((B, M, D), jnp.bfloat16)] * 2,
        compiler_params=pltpu.CompilerParams(
            dimension_semantics=("parallel",),
            vmem_limit_bytes=VMEM_LIMIT_BYTES),
        name="mem_kv",
    )(mem, g, w_kv)


def _mem_attn_kernel(x_ref, g_ref, wq_ref, k_ref, v_ref, wo_ref, o_ref):
    x = x_ref[0]
    d = x.shape[1]
    hd = d // MEM_HEADS
    h = ((x * _rms_scale(x)) * g_ref[...]).astype(jnp.bfloat16)
    q = (_dot(h, wq_ref[...]) * (hd ** -0.5)).astype(jnp.bfloat16)
    outs = []
    for i in range(MEM_HEADS):
        cols = slice(i * hd, (i + 1) * hd)
        sc = _dot_nt(q[:, cols], k_ref[0, :, cols])
        e = jnp.exp(sc - jnp.max(sc, axis=-1, keepdims=True))
        o = _dot(e.astype(jnp.bfloat16), v_ref[0, :, cols])
        outs.append((o / jnp.sum(e, axis=-1, keepdims=True)).astype(jnp.bfloat16))
    o_ref[0] = x + _dot(jnp.concatenate(outs, axis=1), wo_ref[...])


def _mem_attn(x, g, w_q, k_mem, v_mem, w_o):
    B, S, D = x.shape
    M = k_mem.shape[1]
    T = SEQ_TILE
    tile = pl.BlockSpec((1, T, D), lambda b, s: (b, s, 0))
    memb = pl.BlockSpec((1, M, D), lambda b, s: (b, 0, 0))
    full = lambda a: pl.BlockSpec(a.shape, lambda b, s: (0,) * a.ndim)
    return pl.pallas_call(
        _mem_attn_kernel,
        grid=(B, S // T),
        in_specs=[tile, full(g), full(w_q), memb, memb, full(w_o)],
        out_specs=tile,
        out_shape=jax.ShapeDtypeStruct((B, S, D), jnp.float32),
        compiler_params=pltpu.CompilerParams(
            dimension_semantics=("parallel", "parallel"),
            vmem_limit_bytes=VMEM_LIMIT_BYTES),
        name="mem_attn",
    )(x, g, w_q, k_mem, v_mem, w_o)


def _ffn_kernel(x_ref, g_ref, wu_ref, cw_ref, wd_ref, fg_ref, o_ref, tail_ref):
    s = pl.program_id(1)
    dff = cw_ref.shape[1]

    @pl.when(s == 0)
    def _():
        tail_ref[...] = jnp.zeros_like(tail_ref)

    x = x_ref[0]
    h = ((x * _rms_scale(x)) * g_ref[...]).astype(jnp.bfloat16)
    pre = _dot(h, wu_ref[:, :dff])
    gate = _causal_conv3(pre, tail_ref[...], cw_ref[...])
    tail_ref[...] = pre[pre.shape[0] - SUBLANES:]
    u = _dot(h, wu_ref[:, dff:])
    act = (gate * (1.0 / (1.0 + jnp.exp(-gate))) * u).astype(jnp.bfloat16)
    y = x + _dot(act, wd_ref[...])
    o_ref[0] = (y * _rms_scale(y)) * fg_ref[...]


def _ffn(x, g, w_up, conv_w, w_down, final_g):
    B, S, D = x.shape
    T = SEQ_TILE
    dff = conv_w.shape[1]
    tile = pl.BlockSpec((1, T, D), lambda b, s: (b, s, 0))
    full = lambda a: pl.BlockSpec(a.shape, lambda b, s: (0,) * a.ndim)
    return pl.pallas_call(
        _ffn_kernel,
        grid=(B, S // T),
        in_specs=[tile, full(g), full(w_up), full(conv_w), full(w_down), full(final_g)],
        out_specs=tile,
        out_shape=jax.ShapeDtypeStruct((B, S, D), jnp.float32),
        scratch_shapes=[pltpu.VMEM((SUBLANES, dff), jnp.float32)],
        compiler_params=pltpu.CompilerParams(
            dimension_semantics=("parallel", "arbitrary"),
            vmem_limit_bytes=VMEM_LIMIT_BYTES),
        name="ffn",
    )(x, g, w_up, conv_w, w_down, final_g)


def kernel(x, mem, norm_mix_g, w_in_mix, conv_mix_w, attn_out_g, conv_out_g, w_out_mix, norm_mem_g, mem_kv_norm_g, w_mem_q, w_mem_kv, w_mem_o, norm_ffn_g, w_ffn_up, ffn_conv_w, w_ffn_down, final_norm_g):
    depth = w_in_mix.shape[0]
    bf = jnp.bfloat16
    row = lambda a: a.reshape(1, -1)
    for l in range(depth):
        w_in = w_in_mix[l].astype(bf)
        w_vt = w_in[:, 2 * D_ATTN:3 * D_ATTN].T
        q, k, vt, yc, sel = _mix_in(x, row(norm_mix_g[l]), w_in, w_vt,
                                    conv_mix_w[l], row(conv_out_g[l]))
        x = _mix_attn(q, k, vt, sel, yc, x, row(attn_out_g[l]), w_out_mix[l].astype(bf))
        k_mem, v_mem = _mem_kv(mem, row(mem_kv_norm_g[l]), w_mem_kv[l].astype(bf))
        x = _mem_attn(x, row(norm_mem_g[l]), w_mem_q[l].astype(bf), k_mem, v_mem,
                      w_mem_o[l].astype(bf))
        assert depth == 1
        x = _ffn(x, row(norm_ffn_g[l]), w_ffn_up[l].astype(bf), ffn_conv_w[l],
                 w_ffn_down[l].astype(bf), row(final_norm_g))
    return x
```

```python
import jax
import jax.numpy as jnp
from jax import lax
from jax.experimental import pallas as pl
from jax.experimental.pallas import tpu as pltpu

HEAD_DIM = 64
ATTN_HEADS = 8
D_ATTN = ATTN_HEADS * HEAD_DIM
MOBA_BLOCK = 256
MOBA_TOPK = 3
MEM_HEADS = 4
EPS = 1e-6

LANES = 128
SUBLANES = 8
VMEM_LIMIT_BYTES = 56 * 1024 * 1024

SEQ_TILE = MOBA_BLOCK
HEAD_PAIR = 2 * HEAD_DIM
MASKED = -1e30
LOG2E = 1.4426950408889634

_NT = (((1,), (1,)), ((), ()))


def _dot(a, b):
    return jnp.dot(a, b, preferred_element_type=jnp.float32)


def _dot_nt(a, b):
    return lax.dot_general(a, b, _NT, preferred_element_type=jnp.float32)


def _rms_scale(x):
    return lax.rsqrt(jnp.mean(x * x, axis=-1, keepdims=True) + EPS)


def _split_bf16(a):
    hi = a.astype(jnp.bfloat16)
    lo = (a - hi.astype(jnp.float32)).astype(jnp.bfloat16)
    return hi, lo


def _causal_conv3(z, tail, w):
    rows = lax.broadcasted_iota(jnp.int32, (SUBLANES, z.shape[1]), 0)
    z1 = pltpu.roll(z, 1, axis=0)
    z2 = pltpu.roll(z, 2, axis=0)
    t1 = pltpu.roll(tail, 1, axis=0)
    t2 = pltpu.roll(tail, 2, axis=0)
    z1 = jnp.concatenate([jnp.where(rows < 1, t1, z1[:SUBLANES]), z1[SUBLANES:]], axis=0)
    z2 = jnp.concatenate([jnp.where(rows < 2, t2, z2[:SUBLANES]), z2[SUBLANES:]], axis=0)
    return z2 * w[0:1] + z1 * w[1:2] + z * w[2:3]


def _mix_in_kernel(x_ref, g_ref, w_ref, wqvt_ref, cw_ref, cg_ref,
                   qt_ref, k_ref, vt_ref, yc_ref, sel_ref,
                   km_ref, tail_ref):
    s = pl.program_id(1)
    da = D_ATTN
    dc = cw_ref.shape[1]
    nblk = km_ref.shape[0] // ATTN_HEADS

    @pl.when(s == 0)
    def _():
        km_ref[...] = jnp.zeros_like(km_ref)
        tail_ref[...] = jnp.zeros_like(tail_ref)

    x = x_ref[0]
    h = ((x * _rms_scale(x)) * g_ref[...]).astype(jnp.bfloat16)

    o = 3 * da
    cgate = _dot(h, w_ref[:, o + dc:o + 2 * dc])
    u = _dot(h, w_ref[:, o + 2 * dc:o + 3 * dc])
    z = cgate * u
    conv = _causal_conv3(z, tail_ref[...], cw_ref[...])
    tail_ref[...] = z[z.shape[0] - SUBLANES:]
    y = _dot(h, w_ref[:, o:o + dc]) * conv
    yc_ref[0] = ((y * _rms_scale(y)) * cg_ref[...]).astype(jnp.bfloat16)

    qt = _dot_nt(wqvt_ref[:da, :], h)
    vt_ref[0, 0] = _dot_nt(wqvt_ref[da:, :], h).astype(jnp.bfloat16)
    qt_ref[0] = (qt * (HEAD_DIM ** -0.5 * LOG2E)).astype(jnp.bfloat16)

    qh, ql = _split_bf16(qt)
    kmh, kml = _split_bf16(km_ref[...])
    nrow = km_ref.shape[0]
    gate = _dot(jnp.concatenate([kmh, kml], axis=0), qh)
    gate = gate[:nrow] + gate[nrow:] + _dot(kmh, ql)
    g = [jnp.where(n < s, gate[n * ATTN_HEADS:(n + 1) * ATTN_HEADS], -jnp.inf)
         for n in range(nblk)]
    bias = []
    for n in range(nblk):
        rank = jnp.zeros(g[n].shape, jnp.int32)
        for o in range(nblk):
            if o != n:
                beats = (g[o] >= g[n]) if o < n else (g[o] > g[n])
                rank = rank + jnp.where(beats, 1, 0)
        chosen = jnp.logical_and(n < s, rank < MOBA_TOPK)
        bias.append(jnp.where(chosen, 0.0, MASKED).astype(jnp.float32))
    sel_ref[0] = jnp.concatenate(bias, axis=0)

    k = _dot(h, w_ref[:, da:2 * da])
    k_ref[0] = k.astype(jnp.bfloat16)

    kmean = jnp.mean(k, axis=0, keepdims=True)
    row_head = lax.broadcasted_iota(jnp.int32, (ATTN_HEADS, da), 0)
    col_head = lax.broadcasted_iota(jnp.int32, (ATTN_HEADS, da), 1) // HEAD_DIM
    base = pl.multiple_of(s * ATTN_HEADS, ATTN_HEADS)
    km_ref[pl.ds(base, ATTN_HEADS), :] = jnp.where(row_head == col_head, kmean, 0.0)


def _mix_in(x, g, w_in, w_qvt, conv_w, conv_g):
    B, S, D = x.shape
    T = SEQ_TILE
    nb = S // T
    dc = conv_w.shape[1]
    tile = lambda n: pl.BlockSpec((1, T, n), lambda b, s: (b, s, 0))
    full = lambda a: pl.BlockSpec(a.shape, lambda b, s: (0,) * a.ndim)
    bf = jnp.bfloat16
    return pl.pallas_call(
        _mix_in_kernel,
        grid=(B, nb),
        in_specs=[tile(D), full(g), full(w_in), full(w_qvt), full(conv_w), full(conv_g)],
        out_specs=[pl.BlockSpec((1, D_ATTN, T), lambda b, s: (b, 0, s)),
                   tile(D_ATTN),
                   pl.BlockSpec((1, 1, D_ATTN, T), lambda b, s: (b, s, 0, 0)),
                   tile(dc),
                   pl.BlockSpec((1, nb * ATTN_HEADS, T), lambda b, s: (b, 0, s))],
        out_shape=[jax.ShapeDtypeStruct((B, D_ATTN, S), bf),
                   jax.ShapeDtypeStruct((B, S, D_ATTN), bf),
                   jax.ShapeDtypeStruct((B, nb, D_ATTN, T), bf),
                   jax.ShapeDtypeStruct((B, S, dc), bf),
                   jax.ShapeDtypeStruct((B, nb * ATTN_HEADS, S), jnp.float32)],
        scratch_shapes=[pltpu.VMEM((nb * ATTN_HEADS, D_ATTN), jnp.float32),
                        pltpu.VMEM((SUBLANES, dc), jnp.float32)],
        compiler_params=pltpu.CompilerParams(
            dimension_semantics=("parallel", "arbitrary"),
            vmem_limit_bytes=VMEM_LIMIT_BYTES),
        name="mix_in",
    )(x, g, w_in, w_qvt, conv_w, conv_g)


def _mix_attn_kernel(qt_ref, k_ref, vt_ref, sel_ref, yc_ref, x_ref, ag_ref, w_ref,
                     o_ref, qm_ref, s0_ref, mx0_ref, s1_ref, mx1_ref, m_ref, l_ref, acc_ref):
    j = pl.program_id(1)
    T = SEQ_TILE
    nblk = vt_ref.shape[1]

    zeros = jnp.zeros((HEAD_DIM, T), jnp.bfloat16)
    for h in range(ATTN_HEADS):
        qh = qt_ref[0, h * HEAD_DIM:(h + 1) * HEAD_DIM, :]
        qm_ref[h * HEAD_PAIR:(h + 1) * HEAD_PAIR, :] = jnp.concatenate(
            [qh, zeros] if h % 2 == 0 else [zeros, qh], axis=0)

    key = lax.broadcasted_iota(jnp.int32, (T, T), 0)
    qry = lax.broadcasted_iota(jnp.int32, (T, T), 1)
    causal = key <= qry

    def score_head(h, blk, s_ref, mx_ref, own):
        p = h // 2
        start = pl.multiple_of(blk * T, T)
        kp = k_ref[0, pl.ds(start, T), p * HEAD_PAIR:(p + 1) * HEAD_PAIR]
        sc = _dot(kp, qm_ref[h * HEAD_PAIR:(h + 1) * HEAD_PAIR, :])
        if own:
            sc = jnp.where(causal, sc, MASKED)
        s_ref[h] = sc
        mx = jnp.max(sc.reshape(T // SUBLANES, SUBLANES, T), axis=0)
        for sh in (4, 2, 1):
            mx = jnp.maximum(mx, pltpu.roll(mx, sh, axis=0))
        mx_ref[h] = mx

    def softmax_head(h, blk, s_ref, mx_ref, own):
        rows = slice(h * HEAD_DIM, (h + 1) * HEAD_DIM)
        mx = mx_ref[h]
        if own:
            m_new = mx
            shift = mx
        else:
            bias = jnp.broadcast_to(sel_ref[0, pl.ds(blk * ATTN_HEADS + h, 1), :], (SUBLANES, T))
            m_old = m_ref[h]
            m_new = jnp.maximum(m_old, mx + bias)
            alpha = jnp.exp2(m_old - m_new)
            shift = m_new - bias
        e3 = jnp.exp2(s_ref[h].reshape(T // SUBLANES, SUBLANES, T) - shift[None])
        lsum = jnp.sum(e3, axis=0)
        pv = _dot(vt_ref[0, blk, rows, :], e3.reshape(T, T).astype(jnp.bfloat16))
        m_ref[h] = m_new
        if own:
            l_ref[h] = lsum
            acc_ref[rows, :] = pv
        else:
            l_ref[h] = alpha * l_ref[h] + lsum
            acc3 = acc_ref[rows, :].reshape(HEAD_DIM // SUBLANES, SUBLANES, T)
            acc3 = acc3 * alpha[None] + pv.reshape(HEAD_DIM // SUBLANES, SUBLANES, T)
            acc_ref[rows, :] = acc3.reshape(HEAD_DIM, T)

    def stage(cur, blk, own, nxt, nxt_blk):
        for h in range(ATTN_HEADS):
            softmax_head(h, blk, *cur, own)
            if nxt is not None:
                score_head(h, nxt_blk, *nxt, False)

    buf0 = (s0_ref, mx0_ref)
    buf1 = (s1_ref, mx1_ref)

    for h in range(ATTN_HEADS):
        score_head(h, j, *buf0, True)
    stage(buf0, j, True, buf1, 0)

    def past_pair(i, carry):
        n = 2 * i
        stage(buf1, n, False, buf0, n + 1)
        stage(buf0, n + 1, False, buf1, jnp.minimum(n + 2, nblk - 1))
        return carry

    lax.fori_loop(0, j // 2, past_pair, 0)

    @pl.when(j % 2 == 1)
    def _():
        stage(buf1, j - 1, False, None, None)

    yat = jnp.concatenate(
        [acc_ref[h * HEAD_DIM:(h + 1) * HEAD_DIM, :] / jnp.sum(l_ref[h], axis=0, keepdims=True)
         for h in range(ATTN_HEADS)], axis=0)
    ya = yat.T
    yan = ((ya * _rms_scale(ya)) * ag_ref[...]).astype(jnp.bfloat16)
    y = jnp.concatenate([yan, yc_ref[0]], axis=1)
    o_ref[0] = x_ref[0] + _dot(y, w_ref[...])


def _mix_attn(qt, k, vt, sel, yc, x, attn_g, w_out):
    B, S, D = x.shape
    T = SEQ_TILE
    nb = S // T
    tile = lambda n: pl.BlockSpec((1, T, n), lambda b, s: (b, s, 0))
    seq = lambda n: pl.BlockSpec((1, S, n), lambda b, s: (b, 0, 0))
    full = lambda a: pl.BlockSpec(a.shape, lambda b, s: (0,) * a.ndim)
    stat = pltpu.VMEM((ATTN_HEADS, SUBLANES, T), jnp.float32)
    return pl.pallas_call(
        _mix_attn_kernel,
        grid=(B, nb),
        in_specs=[pl.BlockSpec((1, D_ATTN, T), lambda b, s: (b, 0, s)), seq(D_ATTN),
                  pl.BlockSpec((1, nb, D_ATTN, T), lambda b, s: (b, 0, 0, 0)),
                  pl.BlockSpec((1, nb * ATTN_HEADS, T), lambda b, s: (b, 0, s)),
                  tile(yc.shape[2]), tile(D), full(attn_g), full(w_out)],
        out_specs=tile(D),
        out_shape=jax.ShapeDtypeStruct((B, S, D), jnp.float32),
        scratch_shapes=[pltpu.VMEM((ATTN_HEADS * HEAD_PAIR, T), jnp.bfloat16),
                        pltpu.VMEM((ATTN_HEADS, T, T), jnp.float32),
                        stat,
                        pltpu.VMEM((ATTN_HEADS, T, T), jnp.float32),
                        stat, stat, stat,
                        pltpu.VMEM((D_ATTN, T), jnp.float32)],
        compiler_params=pltpu.CompilerParams(
            dimension_semantics=("parallel", "arbitrary"),
            vmem_limit_bytes=VMEM_LIMIT_BYTES),
        name="mix_attn",
    )(qt, k, vt, sel, yc, x, attn_g, w_out)


def _mem_kv_kernel(m_ref, g_ref, w_ref, k_ref, v_ref):
    d = m_ref.shape[2]
    m = m_ref[0]
    h = ((m * _rms_scale(m)) * g_ref[...]).astype(jnp.bfloat16)
    k_ref[0] = _dot(h, w_ref[:, :d]).astype(jnp.bfloat16)
    v_ref[0] = _dot(h, w_ref[:, d:]).astype(jnp.bfloat16)


def _mem_kv(mem, g, w_kv):
    B, M, D = mem.shape
    blk = pl.BlockSpec((1, M, D), lambda b: (b, 0, 0))
    full = lambda a: pl.BlockSpec(a.shape, lambda b: (0,) * a.ndim)
    return pl.pallas_call(
        _mem_kv_kernel,
        grid=(B,),
        in_specs=[blk, full(g), full(w_kv)],
        out_specs=[blk, blk],
        out_shape=[jax.ShapeDtypeStruct((B, M, D), jnp.bfloat16)] * 2,
        compiler_params=pltpu.CompilerParams(
            dimension_semantics=("parallel",),
            vmem_limit_bytes=VMEM_LIMIT_BYTES),
        name="mem_kv",
    )(mem, g, w_kv)


def _mem_attn_kernel(x_ref, g_ref, wq_ref, k_ref, v_ref, wo_ref, o_ref):
    x = x_ref[0]
    d = x.shape[1]
    hd = d // MEM_HEADS
    h = ((x * _rms_scale(x)) * g_ref[...]).astype(jnp.bfloat16)
    q = (_dot(h, wq_ref[...]) * (hd ** -0.5)).astype(jnp.bfloat16)
    outs = []
    for i in range(MEM_HEADS):
        cols = slice(i * hd, (i + 1) * hd)
        sc = _dot_nt(q[:, cols], k_ref[0, :, cols])
        e = jnp.exp(sc - jnp.max(sc, axis=-1, keepdims=True))
        o = _dot(e.astype(jnp.bfloat16), v_ref[0, :, cols])
        outs.append((o / jnp.sum(e, axis=-1, keepdims=True)).astype(jnp.bfloat16))
    o_ref[0] = x + _dot(jnp.concatenate(outs, axis=1), wo_ref[...])


def _mem_attn(x, g, w_q, k_mem, v_mem, w_o):
    B, S, D = x.shape
    M = k_mem.shape[1]
    T = SEQ_TILE
    tile = pl.BlockSpec((1, T, D), lambda b, s: (b, s, 0))
    memb = pl.BlockSpec((1, M, D), lambda b, s: (b, 0, 0))
    full = lambda a: pl.BlockSpec(a.shape, lambda b, s: (0,) * a.ndim)
    return pl.pallas_call(
        _mem_attn_kernel,
        grid=(B, S // T),
        in_specs=[tile, full(g), full(w_q), memb, memb, full(w_o)],
        out_specs=tile,
        out_shape=jax.ShapeDtypeStruct((B, S, D), jnp.float32),
        compiler_params=pltpu.CompilerParams(
            dimension_semantics=("parallel", "parallel"),
            vmem_limit_bytes=VMEM_LIMIT_BYTES),
        name="mem_attn",
    )(x, g, w_q, k_mem, v_mem, w_o)


def _ffn_kernel(x_ref, g_ref, wu_ref, cw_ref, wd_ref, fg_ref, o_ref, tail_ref):
    s = pl.program_id(1)
    dff = cw_ref.shape[1]

    @pl.when(s == 0)
    def _():
        tail_ref[...] = jnp.zeros_like(tail_ref)

    x = x_ref[0]
    h = ((x * _rms_scale(x)) * g_ref[...]).astype(jnp.bfloat16)
    pre = _dot(h, wu_ref[:, :dff])
    gate = _causal_conv3(pre, tail_ref[...], cw_ref[...])
    tail_ref[...] = pre[pre.shape[0] - SUBLANES:]
    u = _dot(h, wu_ref[:, dff:])
    act = (gate * (1.0 / (1.0 + jnp.exp(-gate))) * u).astype(jnp.bfloat16)
    y = x + _dot(act, wd_ref[...])
    o_ref[0] = (y * _rms_scale(y)) * fg_ref[...]


def _ffn(x, g, w_up, conv_w, w_down, final_g):
    B, S, D = x.shape
    T = SEQ_TILE
    dff = conv_w.shape[1]
    tile = pl.BlockSpec((1, T, D), lambda b, s: (b, s, 0))
    full = lambda a: pl.BlockSpec(a.shape, lambda b, s: (0,) * a.ndim)
    return pl.pallas_call(
        _ffn_kernel,
        grid=(B, S // T),
        in_specs=[tile, full(g), full(w_up), full(conv_w), full(w_down), full(final_g)],
        out_specs=tile,
        out_shape=jax.ShapeDtypeStruct((B, S, D), jnp.float32),
        scratch_shapes=[pltpu.VMEM((SUBLANES, dff), jnp.float32)],
        compiler_params=pltpu.CompilerParams(
            dimension_semantics=("parallel", "arbitrary"),
            vmem_limit_bytes=VMEM_LIMIT_BYTES),
        name="ffn",
    )(x, g, w_up, conv_w, w_down, final_g)


def kernel(x, mem, norm_mix_g, w_in_mix, conv_mix_w, attn_out_g, conv_out_g, w_out_mix, norm_mem_g, mem_kv_norm_g, w_mem_q, w_mem_kv, w_mem_o, norm_ffn_g, w_ffn_up, ffn_conv_w, w_ffn_down, final_norm_g):
    depth = w_in_mix.shape[0]
    bf = jnp.bfloat16
    row = lambda a: a.reshape(1, -1)
    for l in range(depth):
        w_in = w_in_mix[l].astype(bf)
        w_qvt = jnp.concatenate([w_in[:, :D_ATTN], w_in[:, 2 * D_ATTN:3 * D_ATTN]], axis=1).T
        qt, k, vt, yc, sel = _mix_in(x, row(norm_mix_g[l]), w_in, w_qvt,
                                     conv_mix_w[l], row(conv_out_g[l]))
        x = _mix_attn(qt, k, vt, sel, yc, x, row(attn_out_g[l]), w_out_mix[l].astype(bf))
        k_mem, v_mem = _mem_kv(mem, row(mem_kv_norm_g[l]), w_mem_kv[l].astype(bf))
        x = _mem_attn(x, row(norm_mem_g[l]), w_mem_q[l].astype(bf), k_mem, v_mem,
                      w_mem_o[l].astype(bf))
        assert depth == 1
        x = _ffn(x, row(norm_ffn_g[l]), w_ffn_up[l].astype(bf), ffn_conv_w[l],
                 w_ffn_down[l].astype(bf), row(final_norm_g))
    return x
```

```python
import jax
import jax.numpy as jnp
from jax import lax
from jax.experimental import pallas as pl
from jax.experimental.pallas import tpu as pltpu

HEAD_DIM = 64
ATTN_HEADS = 8
D_ATTN = ATTN_HEADS * HEAD_DIM
MOBA_BLOCK = 256
MOBA_TOPK = 3
MEM_HEADS = 4
EPS = 1e-6

LANES = 128
SUBLANES = 8
VMEM_LIMIT_BYTES = 56 * 1024 * 1024

SEQ_TILE = MOBA_BLOCK
ROW_TILE = 2 * SEQ_TILE
HEAD_PAIR = 2 * HEAD_DIM
MASKED = -1e30
LOG2E = 1.4426950408889634

_NT = (((1,), (1,)), ((), ()))


def _dot(a, b):
    return jnp.dot(a, b, preferred_element_type=jnp.float32)


def _dot_nt(a, b):
    return lax.dot_general(a, b, _NT, preferred_element_type=jnp.float32)


def _rms_scale(x):
    return lax.rsqrt(jnp.mean(x * x, axis=-1, keepdims=True) + EPS)


def _split_bf16(a):
    hi = a.astype(jnp.bfloat16)
    lo = (a - hi.astype(jnp.float32)).astype(jnp.bfloat16)
    return hi, lo


def _causal_conv3(z, tail, w):
    rows = lax.broadcasted_iota(jnp.int32, (SUBLANES, z.shape[1]), 0)
    z1 = pltpu.roll(z, 1, axis=0)
    z2 = pltpu.roll(z, 2, axis=0)
    t1 = pltpu.roll(tail, 1, axis=0)
    t2 = pltpu.roll(tail, 2, axis=0)
    z1 = jnp.concatenate([jnp.where(rows < 1, t1, z1[:SUBLANES]), z1[SUBLANES:]], axis=0)
    z2 = jnp.concatenate([jnp.where(rows < 2, t2, z2[:SUBLANES]), z2[SUBLANES:]], axis=0)
    return z2 * w[0:1] + z1 * w[1:2] + z * w[2:3]


def _resident(a):
    return pl.BlockSpec(a.shape, lambda *_: (0,) * a.ndim, pipeline_mode=pl.Buffered(1))


def _mix_in_kernel(x_ref, g_ref, w_ref, wqvt_ref, cw_ref, cg_ref,
                   qt_ref, k_ref, vt_ref, yc_ref, sel_ref,
                   km_ref, tail_ref):
    s = pl.program_id(1)
    da = D_ATTN
    dc = cw_ref.shape[1]
    nrow = km_ref.shape[0]
    nblk = nrow // ATTN_HEADS
    T = SEQ_TILE
    groups = x_ref.shape[1] // T

    @pl.when(s == 0)
    def _():
        km_ref[...] = jnp.zeros_like(km_ref)
        tail_ref[...] = jnp.zeros_like(tail_ref)

    row_head = lax.broadcasted_iota(jnp.int32, (ATTN_HEADS, da), 0)
    col_head = lax.broadcasted_iota(jnp.int32, (ATTN_HEADS, da), 1) // HEAD_DIM
    tail = tail_ref[...]

    for gi in range(groups):
        blk = s * groups + gi
        rows = slice(gi * T, (gi + 1) * T)
        x = x_ref[0, rows, :]
        h = ((x * _rms_scale(x)) * g_ref[...]).astype(jnp.bfloat16)

        o = 3 * da
        cgate = _dot(h, w_ref[:, o + dc:o + 2 * dc])
        u = _dot(h, w_ref[:, o + 2 * dc:o + 3 * dc])
        z = cgate * u
        conv = _causal_conv3(z, tail, cw_ref[...])
        tail = z[T - SUBLANES:]
        y = _dot(h, w_ref[:, o:o + dc]) * conv
        yc_ref[0, rows, :] = ((y * _rms_scale(y)) * cg_ref[...]).astype(jnp.bfloat16)

        qt = _dot_nt(wqvt_ref[:da, :], h)
        vt_ref[0, gi] = _dot_nt(wqvt_ref[da:, :], h).astype(jnp.bfloat16)
        qt_ref[0, :, rows] = (qt * (HEAD_DIM ** -0.5 * LOG2E)).astype(jnp.bfloat16)

        qh, ql = _split_bf16(qt)
        kmh, kml = _split_bf16(km_ref[...])
        gate = _dot(jnp.concatenate([kmh, kml], axis=0), qh)
        gate = gate[:nrow] + gate[nrow:] + _dot(kmh, ql)
        g = [jnp.where(n < blk, gate[n * ATTN_HEADS:(n + 1) * ATTN_HEADS], -jnp.inf)
             for n in range(nblk)]
        bias = []
        for n in range(nblk):
            rank = jnp.zeros(g[n].shape, jnp.int32)
            for o in range(nblk):
                if o != n:
                    beats = (g[o] >= g[n]) if o < n else (g[o] > g[n])
                    rank = rank + jnp.where(beats, 1, 0)
            chosen = jnp.logical_and(n < blk, rank < MOBA_TOPK)
            bias.append(jnp.where(chosen, 0.0, MASKED).astype(jnp.float32))
        sel_ref[0, :, rows] = jnp.concatenate(bias, axis=0)

        k = _dot(h, w_ref[:, da:2 * da])
        k_ref[0, rows, :] = k.astype(jnp.bfloat16)

        kmean = jnp.mean(k, axis=0, keepdims=True)
        base = pl.multiple_of(blk * ATTN_HEADS, ATTN_HEADS)
        km_ref[pl.ds(base, ATTN_HEADS), :] = jnp.where(row_head == col_head, kmean, 0.0)

    tail_ref[...] = tail


def _mix_in(x, g, w_in, w_qvt, conv_w, conv_g):
    B, S, D = x.shape
    T = ROW_TILE
    nb = S // SEQ_TILE
    dc = conv_w.shape[1]
    tile = lambda n: pl.BlockSpec((1, T, n), lambda b, s: (b, s, 0))
    bf = jnp.bfloat16
    return pl.pallas_call(
        _mix_in_kernel,
        grid=(B, S // T),
        in_specs=[tile(D), _resident(g), _resident(w_in), _resident(w_qvt),
                  _resident(conv_w), _resident(conv_g)],
        out_specs=[pl.BlockSpec((1, D_ATTN, T), lambda b, s: (b, 0, s)),
                   tile(D_ATTN),
                   pl.BlockSpec((1, T // SEQ_TILE, D_ATTN, SEQ_TILE), lambda b, s: (b, s, 0, 0)),
                   tile(dc),
                   pl.BlockSpec((1, nb * ATTN_HEADS, T), lambda b, s: (b, 0, s))],
        out_shape=[jax.ShapeDtypeStruct((B, D_ATTN, S), bf),
                   jax.ShapeDtypeStruct((B, S, D_ATTN), bf),
                   jax.ShapeDtypeStruct((B, nb, D_ATTN, SEQ_TILE), bf),
                   jax.ShapeDtypeStruct((B, S, dc), bf),
                   jax.ShapeDtypeStruct((B, nb * ATTN_HEADS, S), jnp.float32)],
        scratch_shapes=[pltpu.VMEM((nb * ATTN_HEADS, D_ATTN), jnp.float32),
                        pltpu.VMEM((SUBLANES, dc), jnp.float32)],
        compiler_params=pltpu.CompilerParams(
            dimension_semantics=("parallel", "arbitrary"),
            vmem_limit_bytes=VMEM_LIMIT_BYTES),
        name="mix_in",
    )(x, g, w_in, w_qvt, conv_w, conv_g)


def _mix_attn_kernel(qt_ref, k_ref, vt_ref, sel_ref, yc_ref, x_ref, ag_ref, w_ref,
                     o_ref, qm_ref, s0_ref, mx0_ref, s1_ref, mx1_ref, m_ref, l_ref, acc_ref):
    j = pl.program_id(1)
    T = SEQ_TILE
    nblk = vt_ref.shape[1]

    zeros = jnp.zeros((HEAD_DIM, T), jnp.bfloat16)
    for h in range(ATTN_HEADS):
        qh = qt_ref[0, h * HEAD_DIM:(h + 1) * HEAD_DIM, :]
        qm_ref[h * HEAD_PAIR:(h + 1) * HEAD_PAIR, :] = jnp.concatenate(
            [qh, zeros] if h % 2 == 0 else [zeros, qh], axis=0)

    key = lax.broadcasted_iota(jnp.int32, (T, T), 0)
    qry = lax.broadcasted_iota(jnp.int32, (T, T), 1)
    causal = key <= qry

    def score_head(h, blk, s_ref, mx_ref, own):
        p = h // 2
        start = pl.multiple_of(blk * T, T)
        kp = k_ref[0, pl.ds(start, T), p * HEAD_PAIR:(p + 1) * HEAD_PAIR]
        sc = _dot(kp, qm_ref[h * HEAD_PAIR:(h + 1) * HEAD_PAIR, :])
        if own:
            sc = jnp.where(causal, sc, MASKED)
        s_ref[h] = sc
        mx = jnp.max(sc.reshape(T // SUBLANES, SUBLANES, T), axis=0)
        for sh in (4, 2, 1):
            mx = jnp.maximum(mx, pltpu.roll(mx, sh, axis=0))
        mx_ref[h] = mx

    def softmax_head(h, blk, s_ref, mx_ref, own):
        rows = slice(h * HEAD_DIM, (h + 1) * HEAD_DIM)
        mx = mx_ref[h]
        if own:
            m_new = mx
            shift = mx
        else:
            bias = jnp.broadcast_to(sel_ref[0, pl.ds(blk * ATTN_HEADS + h, 1), :], (SUBLANES, T))
            m_old = m_ref[h]
            m_new = jnp.maximum(m_old, mx + bias)
            alpha = jnp.exp2(m_old - m_new)
            shift = m_new - bias
        e3 = jnp.exp2(s_ref[h].reshape(T // SUBLANES, SUBLANES, T) - shift[None])
        lsum = jnp.sum(e3, axis=0)
        pv = _dot(vt_ref[0, blk, rows, :], e3.reshape(T, T).astype(jnp.bfloat16))
        m_ref[h] = m_new
        if own:
            l_ref[h] = lsum
            acc_ref[rows, :] = pv
        else:
            l_ref[h] = alpha * l_ref[h] + lsum
            acc3 = acc_ref[rows, :].reshape(HEAD_DIM // SUBLANES, SUBLANES, T)
            acc3 = acc3 * alpha[None] + pv.reshape(HEAD_DIM // SUBLANES, SUBLANES, T)
            acc_ref[rows, :] = acc3.reshape(HEAD_DIM, T)

    def stage(cur, blk, own, nxt, nxt_blk):
        for h in range(ATTN_HEADS):
            softmax_head(h, blk, *cur, own)
            if nxt is not None:
                score_head(h, nxt_blk, *nxt, False)

    buf0 = (s0_ref, mx0_ref)
    buf1 = (s1_ref, mx1_ref)

    for h in range(ATTN_HEADS):
        score_head(h, j, *buf0, True)
    stage(buf0, j, True, buf1, 0)

    def past_pair(i, carry):
        n = 2 * i
        stage(buf1, n, False, buf0, n + 1)
        stage(buf0, n + 1, False, buf1, jnp.minimum(n + 2, nblk - 1))
        return carry

    lax.fori_loop(0, j // 2, past_pair, 0)

    @pl.when(j % 2 == 1)
    def _():
        stage(buf1, j - 1, False, None, None)

    yat = jnp.concatenate(
        [acc_ref[h * HEAD_DIM:(h + 1) * HEAD_DIM, :] / jnp.sum(l_ref[h], axis=0, keepdims=True)
         for h in range(ATTN_HEADS)], axis=0)
    ya = yat.T
    yan = ((ya * _rms_scale(ya)) * ag_ref[...]).astype(jnp.bfloat16)
    y = jnp.concatenate([yan, yc_ref[0]], axis=1)
    o_ref[0] = x_ref[0] + _dot(y, w_ref[...])


def _mix_attn(qt, k, vt, sel, yc, x, attn_g, w_out):
    B, S, D = x.shape
    T = SEQ_TILE
    nb = S // T
    tile = lambda n: pl.BlockSpec((1, T, n), lambda b, s: (b, s, 0))
    seq = lambda n: pl.BlockSpec((1, S, n), lambda b, s: (b, 0, 0))
    stat = pltpu.VMEM((ATTN_HEADS, SUBLANES, T), jnp.float32)
    return pl.pallas_call(
        _mix_attn_kernel,
        grid=(B, nb),
        in_specs=[pl.BlockSpec((1, D_ATTN, T), lambda b, s: (b, 0, s)), seq(D_ATTN),
                  pl.BlockSpec((1, nb, D_ATTN, T), lambda b, s: (b, 0, 0, 0)),
                  pl.BlockSpec((1, nb * ATTN_HEADS, T), lambda b, s: (b, 0, s)),
                  tile(yc.shape[2]), tile(D), _resident(attn_g), _resident(w_out)],
        out_specs=tile(D),
        out_shape=jax.ShapeDtypeStruct((B, S, D), jnp.float32),
        scratch_shapes=[pltpu.VMEM((ATTN_HEADS * HEAD_PAIR, T), jnp.bfloat16),
                        pltpu.VMEM((ATTN_HEADS, T, T), jnp.float32),
                        stat,
                        pltpu.VMEM((ATTN_HEADS, T, T), jnp.float32),
                        stat, stat, stat,
                        pltpu.VMEM((D_ATTN, T), jnp.float32)],
        compiler_params=pltpu.CompilerParams(
            dimension_semantics=("parallel", "arbitrary"),
            vmem_limit_bytes=VMEM_LIMIT_BYTES),
        name="mix_attn",
    )(qt, k, vt, sel, yc, x, attn_g, w_out)


def _mem_kv_kernel(m_ref, g_ref, w_ref, k_ref, v_ref):
    d = m_ref.shape[2]
    m = m_ref[0]
    h = ((m * _rms_scale(m)) * g_ref[...]).astype(jnp.bfloat16)
    k_ref[0] = _dot(h, w_ref[:, :d]).astype(jnp.bfloat16)
    v_ref[0] = _dot(h, w_ref[:, d:]).astype(jnp.bfloat16)


def _mem_kv(mem, g, w_kv):
    B, M, D = mem.shape
    blk = pl.BlockSpec((1, M, D), lambda b: (b, 0, 0))
    return pl.pallas_call(
        _mem_kv_kernel,
        grid=(B,),
        in_specs=[blk, _resident(g), _resident(w_kv)],
        out_specs=[blk, blk],
        out_shape=[jax.ShapeDtypeStruct((B, M, D), jnp.bfloat16)] * 2,
        compiler_params=pltpu.CompilerParams(
            dimension_semantics=("parallel",),
            vmem_limit_bytes=VMEM_LIMIT_BYTES),
        name="mem_kv",
    )(mem, g, w_kv)


def _mem_attn_kernel(x_ref, g_ref, wq_ref, k_ref, v_ref, wo_ref, o_ref):
    d = x_ref.shape[2]
    hd = d // MEM_HEADS
    heads = [slice(i * hd, (i + 1) * hd) for i in range(MEM_HEADS)]

    def scores(r):
        x = x_ref[0, r:r + SEQ_TILE, :]
        h = ((x * _rms_scale(x)) * g_ref[...]).astype(jnp.bfloat16)
        q = (_dot(h, wq_ref[...]) * (hd ** -0.5 * LOG2E)).astype(jnp.bfloat16)
        return [_dot_nt(q[:, c], k_ref[0, :, c]) for c in heads]

    groups = list(range(0, x_ref.shape[1], SEQ_TILE))
    sc_next = scores(groups[0])
    for gi, r in enumerate(groups):
        sc = sc_next
        if gi + 1 < len(groups):
            sc_next = scores(groups[gi + 1])
        outs = []
        for s_h, c in zip(sc, heads):
            e = jnp.exp2(s_h - jnp.max(s_h, axis=-1, keepdims=True))
            o = _dot(e.astype(jnp.bfloat16), v_ref[0, :, c])
            outs.append((o / jnp.sum(e, axis=-1, keepdims=True)).astype(jnp.bfloat16))
        o_ref[0, r:r + SEQ_TILE, :] = (x_ref[0, r:r + SEQ_TILE, :]
                                       + _dot(jnp.concatenate(outs, axis=1), wo_ref[...]))


def _mem_attn(x, g, w_q, k_mem, v_mem, w_o):
    B, S, D = x.shape
    M = k_mem.shape[1]
    T = ROW_TILE
    tile = pl.BlockSpec((1, T, D), lambda b, s: (b, s, 0))
    memb = pl.BlockSpec((1, M, D), lambda b, s: (b, 0, 0))
    return pl.pallas_call(
        _mem_attn_kernel,
        grid=(B, S // T),
        in_specs=[tile, _resident(g), _resident(w_q), memb, memb, _resident(w_o)],
        out_specs=tile,
        out_shape=jax.ShapeDtypeStruct((B, S, D), jnp.float32),
        compiler_params=pltpu.CompilerParams(
            dimension_semantics=("parallel", "parallel"),
            vmem_limit_bytes=VMEM_LIMIT_BYTES),
        name="mem_attn",
    )(x, g, w_q, k_mem, v_mem, w_o)


def _ffn_kernel(x_ref, g_ref, wu_ref, cw_ref, wd_ref, fg_ref, o_ref, tail_ref):
    s = pl.program_id(1)
    dff = cw_ref.shape[1]

    @pl.when(s == 0)
    def _():
        tail_ref[...] = jnp.zeros_like(tail_ref)

    tail = tail_ref[...]
    for r in range(0, x_ref.shape[1], SEQ_TILE):
        x = x_ref[0, r:r + SEQ_TILE, :]
        h = ((x * _rms_scale(x)) * g_ref[...]).astype(jnp.bfloat16)
        pre = _dot(h, wu_ref[:, :dff])
        gate = _causal_conv3(pre, tail, cw_ref[...])
        tail = pre[pre.shape[0] - SUBLANES:]
        u = _dot(h, wu_ref[:, dff:])
        act = (gate * (1.0 / (1.0 + jnp.exp(-gate))) * u).astype(jnp.bfloat16)
        y = x + _dot(act, wd_ref[...])
        o_ref[0, r:r + SEQ_TILE, :] = (y * _rms_scale(y)) * fg_ref[...]
    tail_ref[...] = tail


def _ffn(x, g, w_up, conv_w, w_down, final_g):
    B, S, D = x.shape
    T = ROW_TILE
    dff = conv_w.shape[1]
    tile = pl.BlockSpec((1, T, D), lambda b, s: (b, s, 0))
    return pl.pallas_call(
        _ffn_kernel,
        grid=(B, S // T),
        in_specs=[tile, _resident(g), _resident(w_up), _resident(conv_w),
                  _resident(w_down), _resident(final_g)],
        out_specs=tile,
        out_shape=jax.ShapeDtypeStruct((B, S, D), jnp.float32),
        scratch_shapes=[pltpu.VMEM((SUBLANES, dff), jnp.float32)],
        compiler_params=pltpu.CompilerParams(
            dimension_semantics=("parallel", "arbitrary"),
            vmem_limit_bytes=VMEM_LIMIT_BYTES),
        name="ffn",
    )(x, g, w_up, conv_w, w_down, final_g)


def kernel(x, mem, norm_mix_g, w_in_mix, conv_mix_w, attn_out_g, conv_out_g, w_out_mix, norm_mem_g, mem_kv_norm_g, w_mem_q, w_mem_kv, w_mem_o, norm_ffn_g, w_ffn_up, ffn_conv_w, w_ffn_down, final_norm_g):
    depth = w_in_mix.shape[0]
    bf = jnp.bfloat16
    row = lambda a: a.reshape(1, -1)
    for l in range(depth):
        w_in = w_in_mix[l].astype(bf)
        w_qvt = jnp.concatenate([w_in[:, :D_ATTN], w_in[:, 2 * D_ATTN:3 * D_ATTN]], axis=1).T
        qt, k, vt, yc, sel = _mix_in(x, row(norm_mix_g[l]), w_in, w_qvt,
                                     conv_mix_w[l], row(conv_out_g[l]))
        x = _mix_attn(qt, k, vt, sel, yc, x, row(attn_out_g[l]), w_out_mix[l].astype(bf))
        k_mem, v_mem = _mem_kv(mem, row(mem_kv_norm_g[l]), w_mem_kv[l].astype(bf))
        x = _mem_attn(x, row(norm_mem_g[l]), w_mem_q[l].astype(bf), k_mem, v_mem,
                      w_mem_o[l].astype(bf))
        assert depth == 1
        x = _ffn(x, row(norm_ffn_g[l]), w_ffn_up[l].astype(bf), ffn_conv_w[l],
                 w_ffn_down[l].astype(bf), row(final_norm_g))
    return x
```

```python
import jax
import jax.numpy as jnp
from jax import lax
from jax.experimental import pallas as pl
from jax.experimental.pallas import tpu as pltpu

HEAD_DIM = 64
ATTN_HEADS = 8
D_ATTN = ATTN_HEADS * HEAD_DIM
MOBA_BLOCK = 256
MOBA_TOPK = 3
MEM_HEADS = 4
EPS = 1e-6

LANES = 128
SUBLANES = 8
VMEM_LIMIT_BYTES = 56 * 1024 * 1024

SEQ_TILE = MOBA_BLOCK
ROW_TILE = 4 * SEQ_TILE
HEAD_PAIR = 2 * HEAD_DIM
MASKED = -1e30
LOG2E = 1.4426950408889634

_NT = (((1,), (1,)), ((), ()))


def _dot(a, b):
    return jnp.dot(a, b, preferred_element_type=jnp.float32)


def _dot_nt(a, b):
    return lax.dot_general(a, b, _NT, preferred_element_type=jnp.float32)


def _rms_scale(x):
    return lax.rsqrt(jnp.mean(x * x, axis=-1, keepdims=True) + EPS)


def _split_bf16(a):
    hi = a.astype(jnp.bfloat16)
    lo = (a - hi.astype(jnp.float32)).astype(jnp.bfloat16)
    return hi, lo


def _causal_conv3(z, tail, w):
    rows = lax.broadcasted_iota(jnp.int32, (SUBLANES, z.shape[1]), 0)
    z1 = pltpu.roll(z, 1, axis=0)
    z2 = pltpu.roll(z, 2, axis=0)
    t1 = pltpu.roll(tail, 1, axis=0)
    t2 = pltpu.roll(tail, 2, axis=0)
    z1 = jnp.concatenate([jnp.where(rows < 1, t1, z1[:SUBLANES]), z1[SUBLANES:]], axis=0)
    z2 = jnp.concatenate([jnp.where(rows < 2, t2, z2[:SUBLANES]), z2[SUBLANES:]], axis=0)
    return z2 * w[0:1] + z1 * w[1:2] + z * w[2:3]


def _resident(a):
    return pl.BlockSpec(a.shape, lambda *_: (0,) * a.ndim, pipeline_mode=pl.Buffered(1))


def _mix_in_kernel(x_ref, g_ref, w_ref, wqvt_ref, cw_ref, cg_ref,
                   qt_ref, k_ref, vt_ref, yc_ref, sel_ref,
                   km_ref, tail_ref):
    s = pl.program_id(1)
    da = D_ATTN
    dc = cw_ref.shape[1]
    nrow = km_ref.shape[0]
    nblk = nrow // ATTN_HEADS
    T = SEQ_TILE
    groups = x_ref.shape[1] // T

    @pl.when(s == 0)
    def _():
        km_ref[...] = jnp.zeros_like(km_ref)
        tail_ref[...] = jnp.zeros_like(tail_ref)

    row_head = lax.broadcasted_iota(jnp.int32, (ATTN_HEADS, da), 0)
    col_head = lax.broadcasted_iota(jnp.int32, (ATTN_HEADS, da), 1) // HEAD_DIM
    tail = tail_ref[...]

    for gi in range(groups):
        blk = s * groups + gi
        rows = slice(gi * T, (gi + 1) * T)
        x = x_ref[0, rows, :]
        h = ((x * _rms_scale(x)) * g_ref[...]).astype(jnp.bfloat16)

        o = 3 * da
        cgate = _dot(h, w_ref[:, o + dc:o + 2 * dc])
        u = _dot(h, w_ref[:, o + 2 * dc:o + 3 * dc])
        z = cgate * u
        conv = _causal_conv3(z, tail, cw_ref[...])
        tail = z[T - SUBLANES:]
        y = _dot(h, w_ref[:, o:o + dc]) * conv
        yc_ref[0, rows, :] = ((y * _rms_scale(y)) * cg_ref[...]).astype(jnp.bfloat16)

        qt = _dot_nt(wqvt_ref[:da, :], h)
        vt_ref[0, gi] = _dot_nt(wqvt_ref[da:, :], h).astype(jnp.bfloat16)
        qt_ref[0, :, rows] = (qt * (HEAD_DIM ** -0.5 * LOG2E)).astype(jnp.bfloat16)

        qh, ql = _split_bf16(qt)
        kmh, kml = _split_bf16(km_ref[...])
        gate = _dot(jnp.concatenate([kmh, kml], axis=0), qh)
        gate = gate[:nrow] + gate[nrow:] + _dot(kmh, ql)
        g = [jnp.where(n < blk, gate[n * ATTN_HEADS:(n + 1) * ATTN_HEADS], -jnp.inf)
             for n in range(nblk)]
        bias = []
        for n in range(nblk):
            rank = jnp.zeros(g[n].shape, jnp.int32)
            for o in range(nblk):
                if o != n:
                    beats = (g[o] >= g[n]) if o < n else (g[o] > g[n])
                    rank = rank + jnp.where(beats, 1, 0)
            chosen = jnp.logical_and(n < blk, rank < MOBA_TOPK)
            bias.append(jnp.where(chosen, 0.0, MASKED).astype(jnp.float32))
        sel_ref[0, :, rows] = jnp.concatenate(bias, axis=0)

        k = _dot(h, w_ref[:, da:2 * da])
        k_ref[0, rows, :] = k.astype(jnp.bfloat16)

        kmean = jnp.mean(k, axis=0, keepdims=True)
        base = pl.multiple_of(blk * ATTN_HEADS, ATTN_HEADS)
        km_ref[pl.ds(base, ATTN_HEADS), :] = jnp.where(row_head == col_head, kmean, 0.0)

    tail_ref[...] = tail


def _mix_in(x, g, w_in, w_qvt, conv_w, conv_g):
    B, S, D = x.shape
    T = ROW_TILE
    nb = S // SEQ_TILE
    dc = conv_w.shape[1]
    tile = lambda n: pl.BlockSpec((1, T, n), lambda b, s: (b, s, 0))
    bf = jnp.bfloat16
    return pl.pallas_call(
        _mix_in_kernel,
        grid=(B, S // T),
        in_specs=[tile(D), _resident(g), _resident(w_in), _resident(w_qvt),
                  _resident(conv_w), _resident(conv_g)],
        out_specs=[pl.BlockSpec((1, D_ATTN, T), lambda b, s: (b, 0, s)),
                   tile(D_ATTN),
                   pl.BlockSpec((1, T // SEQ_TILE, D_ATTN, SEQ_TILE), lambda b, s: (b, s, 0, 0)),
                   tile(dc),
                   pl.BlockSpec((1, nb * ATTN_HEADS, T), lambda b, s: (b, 0, s))],
        out_shape=[jax.ShapeDtypeStruct((B, D_ATTN, S), bf),
                   jax.ShapeDtypeStruct((B, S, D_ATTN), bf),
                   jax.ShapeDtypeStruct((B, nb, D_ATTN, SEQ_TILE), bf),
                   jax.ShapeDtypeStruct((B, S, dc), bf),
                   jax.ShapeDtypeStruct((B, nb * ATTN_HEADS, S), jnp.float32)],
        scratch_shapes=[pltpu.VMEM((nb * ATTN_HEADS, D_ATTN), jnp.float32),
                        pltpu.VMEM((SUBLANES, dc), jnp.float32)],
        compiler_params=pltpu.CompilerParams(
            dimension_semantics=("parallel", "arbitrary"),
            vmem_limit_bytes=VMEM_LIMIT_BYTES),
        name="mix_in",
    )(x, g, w_in, w_qvt, conv_w, conv_g)


def _mix_attn_kernel(qt_ref, k_ref, vt_ref, sel_ref, yc_ref, x_ref, ag_ref, w_ref,
                     o_ref, qm_ref, s0_ref, mx0_ref, s1_ref, mx1_ref, m_ref, l_ref, acc_ref):
    j = pl.program_id(1)
    T = SEQ_TILE
    nblk = vt_ref.shape[1]

    zeros = jnp.zeros((HEAD_DIM, T), jnp.bfloat16)
    ones = jnp.ones((2 * SUBLANES, T), jnp.bfloat16)
    for h in range(ATTN_HEADS):
        qh = qt_ref[0, h * HEAD_DIM:(h + 1) * HEAD_DIM, :]
        qm_ref[h * HEAD_PAIR:(h + 1) * HEAD_PAIR, :] = jnp.concatenate(
            [qh, zeros] if h % 2 == 0 else [zeros, qh], axis=0)

    key = lax.broadcasted_iota(jnp.int32, (T, T), 0)
    qry = lax.broadcasted_iota(jnp.int32, (T, T), 1)
    causal = key <= qry

    def score_head(h, blk, s_ref, mx_ref, own):
        p = h // 2
        start = pl.multiple_of(blk * T, T)
        kp = k_ref[0, pl.ds(start, T), p * HEAD_PAIR:(p + 1) * HEAD_PAIR]
        sc = _dot(kp, qm_ref[h * HEAD_PAIR:(h + 1) * HEAD_PAIR, :])
        if own:
            sc = jnp.where(causal, sc, MASKED)
        s_ref[h] = sc
        mx = jnp.max(sc.reshape(T // SUBLANES, SUBLANES, T), axis=0)
        for sh in (4, 2, 1):
            mx = jnp.maximum(mx, pltpu.roll(mx, sh, axis=0))
        mx_ref[h] = mx

    def softmax_head(h, blk, s_ref, mx_ref, own):
        rows = slice(h * HEAD_DIM, (h + 1) * HEAD_DIM)
        mx = mx_ref[h]
        if own:
            m_new = mx
            shift = mx
        else:
            bias = jnp.broadcast_to(sel_ref[0, pl.ds(blk * ATTN_HEADS + h, 1), :], (SUBLANES, T))
            m_old = m_ref[h]
            m_new = jnp.maximum(m_old, mx + bias)
            alpha = jnp.exp2(m_old - m_new)
            shift = m_new - bias
        e = jnp.exp2(s_ref[h].reshape(T // SUBLANES, SUBLANES, T) - shift[None])
        lhs = jnp.concatenate([vt_ref[0, blk, rows, :], ones], axis=0)
        pv = _dot(lhs, e.reshape(T, T).astype(jnp.bfloat16))
        lsum = pv[HEAD_DIM:HEAD_DIM + SUBLANES]
        pv = pv[:HEAD_DIM]
        m_ref[h] = m_new
        if own:
            l_ref[h] = lsum
            acc_ref[rows, :] = pv
        else:
            l_ref[h] = alpha * l_ref[h] + lsum
            acc3 = acc_ref[rows, :].reshape(HEAD_DIM // SUBLANES, SUBLANES, T)
            acc3 = acc3 * alpha[None] + pv.reshape(HEAD_DIM // SUBLANES, SUBLANES, T)
            acc_ref[rows, :] = acc3.reshape(HEAD_DIM, T)

    def stage(cur, blk, own, nxt, nxt_blk):
        for h in range(ATTN_HEADS):
            softmax_head(h, blk, *cur, own)
            if nxt is not None:
                score_head(h, nxt_blk, *nxt, False)

    buf0 = (s0_ref, mx0_ref)
    buf1 = (s1_ref, mx1_ref)

    for h in range(ATTN_HEADS):
        score_head(h, j, *buf0, True)
    stage(buf0, j, True, buf1, 0)

    def past_pair(i, carry):
        n = 2 * i
        stage(buf1, n, False, buf0, n + 1)
        stage(buf0, n + 1, False, buf1, jnp.minimum(n + 2, nblk - 1))
        return carry

    lax.fori_loop(0, j // 2, past_pair, 0)

    @pl.when(j % 2 == 1)
    def _():
        stage(buf1, j - 1, False, None, None)

    yat = jnp.concatenate(
        [acc_ref[h * HEAD_DIM:(h + 1) * HEAD_DIM, :] / l_ref[h, 0:1, :]
         for h in range(ATTN_HEADS)], axis=0)
    scale = lax.rsqrt(jnp.mean(yat * yat, axis=0, keepdims=True) + EPS)
    yant = ((yat * scale) * ag_ref[...]).astype(jnp.bfloat16)
    proj = lax.dot_general(yant, w_ref[:D_ATTN, :], (((0,), (0,)), ((), ())),
                           preferred_element_type=jnp.float32)
    o_ref[0] = x_ref[0] + proj + _dot(yc_ref[0], w_ref[D_ATTN:, :])


def _mix_attn(qt, k, vt, sel, yc, x, attn_g, w_out):
    B, S, D = x.shape
    T = SEQ_TILE
    nb = S // T
    tile = lambda n: pl.BlockSpec((1, T, n), lambda b, s: (b, s, 0))
    seq = lambda n: pl.BlockSpec((1, S, n), lambda b, s: (b, 0, 0))
    stat = pltpu.VMEM((ATTN_HEADS, SUBLANES, T), jnp.float32)
    return pl.pallas_call(
        _mix_attn_kernel,
        grid=(B, nb),
        in_specs=[pl.BlockSpec((1, D_ATTN, T), lambda b, s: (b, 0, s)), seq(D_ATTN),
                  pl.BlockSpec((1, nb, D_ATTN, T), lambda b, s: (b, 0, 0, 0)),
                  pl.BlockSpec((1, nb * ATTN_HEADS, T), lambda b, s: (b, 0, s)),
                  tile(yc.shape[2]), tile(D), _resident(attn_g), _resident(w_out)],
        out_specs=tile(D),
        out_shape=jax.ShapeDtypeStruct((B, S, D), jnp.float32),
        scratch_shapes=[pltpu.VMEM((ATTN_HEADS * HEAD_PAIR, T), jnp.bfloat16),
                        pltpu.VMEM((ATTN_HEADS, T, T), jnp.float32),
                        stat,
                        pltpu.VMEM((ATTN_HEADS, T, T), jnp.float32),
                        stat, stat, stat,
                        pltpu.VMEM((D_ATTN, T), jnp.float32)],
        compiler_params=pltpu.CompilerParams(
            dimension_semantics=("parallel", "arbitrary"),
            vmem_limit_bytes=VMEM_LIMIT_BYTES),
        name="mix_attn",
    )(qt, k, vt, sel, yc, x, attn_g, w_out)


def _mem_kv_kernel(m_ref, g_ref, w_ref, k_ref, v_ref):
    d = m_ref.shape[2]
    m = m_ref[0]
    h = ((m * _rms_scale(m)) * g_ref[...]).astype(jnp.bfloat16)
    k_ref[0] = _dot(h, w_ref[:, :d]).astype(jnp.bfloat16)
    v_ref[0] = _dot(h, w_ref[:, d:]).astype(jnp.bfloat16)


def _mem_kv(mem, g, w_kv):
    B, M, D = mem.shape
    blk = pl.BlockSpec((1, M, D), lambda b: (b, 0, 0))
    return pl.pallas_call(
        _mem_kv_kernel,
        grid=(B,),
        in_specs=[blk, _resident(g), _resident(w_kv)],
        out_specs=[blk, blk],
        out_shape=[jax.ShapeDtypeStruct((B, M, D), jnp.bfloat16)] * 2,
        compiler_params=pltpu.CompilerParams(
            dimension_semantics=("parallel",),
            vmem_limit_bytes=VMEM_LIMIT_BYTES),
        name="mem_kv",
    )(mem, g, w_kv)


def _mem_attn_kernel(x_ref, g_ref, wq_ref, k_ref, v_ref, wo_ref, o_ref):
    d = x_ref.shape[2]
    hd = d // MEM_HEADS
    heads = [slice(i * hd, (i + 1) * hd) for i in range(MEM_HEADS)]

    def scores(r):
        x = x_ref[0, r:r + SEQ_TILE, :]
        h = ((x * _rms_scale(x)) * g_ref[...]).astype(jnp.bfloat16)
        q = (_dot(h, wq_ref[...]) * (hd ** -0.5 * LOG2E)).astype(jnp.bfloat16)
        return [_dot_nt(q[:, c], k_ref[0, :, c]) for c in heads]

    groups = list(range(0, x_ref.shape[1], SEQ_TILE))
    sc_next = scores(groups[0])
    for gi, r in enumerate(groups):
        sc = sc_next
        if gi + 1 < len(groups):
            sc_next = scores(groups[gi + 1])
        outs = []
        for s_h, c in zip(sc, heads):
            e = jnp.exp2(s_h - jnp.max(s_h, axis=-1, keepdims=True))
            o = _dot(e.astype(jnp.bfloat16), v_ref[0, :, c])
            outs.append((o / jnp.sum(e, axis=-1, keepdims=True)).astype(jnp.bfloat16))
        o_ref[0, r:r + SEQ_TILE, :] = (x_ref[0, r:r + SEQ_TILE, :]
                                       + _dot(jnp.concatenate(outs, axis=1), wo_ref[...]))


def _mem_attn(x, g, w_q, k_mem, v_mem, w_o):
    B, S, D = x.shape
    M = k_mem.shape[1]
    T = ROW_TILE
    tile = pl.BlockSpec((1, T, D), lambda b, s: (b, s, 0))
    memb = pl.BlockSpec((1, M, D), lambda b, s: (b, 0, 0))
    return pl.pallas_call(
        _mem_attn_kernel,
        grid=(B, S // T),
        in_specs=[tile, _resident(g), _resident(w_q), memb, memb, _resident(w_o)],
        out_specs=tile,
        out_shape=jax.ShapeDtypeStruct((B, S, D), jnp.float32),
        compiler_params=pltpu.CompilerParams(
            dimension_semantics=("parallel", "parallel"),
            vmem_limit_bytes=VMEM_LIMIT_BYTES),
        name="mem_attn",
    )(x, g, w_q, k_mem, v_mem, w_o)


def _ffn_kernel(x_ref, g_ref, wu_ref, cw_ref, wd_ref, fg_ref, o_ref, tail_ref):
    s = pl.program_id(1)
    dff = cw_ref.shape[1]

    @pl.when(s == 0)
    def _():
        tail_ref[...] = jnp.zeros_like(tail_ref)

    tail = tail_ref[...]
    for r in range(0, x_ref.shape[1], SEQ_TILE):
        x = x_ref[0, r:r + SEQ_TILE, :]
        h = ((x * _rms_scale(x)) * g_ref[...]).astype(jnp.bfloat16)
        pre = _dot(h, wu_ref[:, :dff])
        gate = _causal_conv3(pre, tail, cw_ref[...])
        tail = pre[pre.shape[0] - SUBLANES:]
        u = _dot(h, wu_ref[:, dff:])
        act = (gate * (1.0 / (1.0 + jnp.exp(-gate))) * u).astype(jnp.bfloat16)
        y = x + _dot(act, wd_ref[...])
        o_ref[0, r:r + SEQ_TILE, :] = (y * _rms_scale(y)) * fg_ref[...]
    tail_ref[...] = tail


def _ffn(x, g, w_up, conv_w, w_down, final_g):
    B, S, D = x.shape
    T = ROW_TILE
    dff = conv_w.shape[1]
    tile = pl.BlockSpec((1, T, D), lambda b, s: (b, s, 0))
    return pl.pallas_call(
        _ffn_kernel,
        grid=(B, S // T),
        in_specs=[tile, _resident(g), _resident(w_up), _resident(conv_w),
                  _resident(w_down), _resident(final_g)],
        out_specs=tile,
        out_shape=jax.ShapeDtypeStruct((B, S, D), jnp.float32),
        scratch_shapes=[pltpu.VMEM((SUBLANES, dff), jnp.float32)],
        compiler_params=pltpu.CompilerParams(
            dimension_semantics=("parallel", "arbitrary"),
            vmem_limit_bytes=VMEM_LIMIT_BYTES),
        name="ffn",
    )(x, g, w_up, conv_w, w_down, final_g)


def kernel(x, mem, norm_mix_g, w_in_mix, conv_mix_w, attn_out_g, conv_out_g, w_out_mix, norm_mem_g, mem_kv_norm_g, w_mem_q, w_mem_kv, w_mem_o, norm_ffn_g, w_ffn_up, ffn_conv_w, w_ffn_down, final_norm_g):
    depth = w_in_mix.shape[0]
    bf = jnp.bfloat16
    row = lambda a: a.reshape(1, -1)
    for l in range(depth):
        w_in = w_in_mix[l].astype(bf)
        w_qvt = jnp.concatenate([w_in[:, :D_ATTN], w_in[:, 2 * D_ATTN:3 * D_ATTN]], axis=1).T
        qt, k, vt, yc, sel = _mix_in(x, row(norm_mix_g[l]), w_in, w_qvt,
                                     conv_mix_w[l], row(conv_out_g[l]))
        attn_g_cols = jnp.broadcast_to(attn_out_g[l][:, None], (D_ATTN, SEQ_TILE))
        x = _mix_attn(qt, k, vt, sel, yc, x, attn_g_cols, w_out_mix[l].astype(bf))
        k_mem, v_mem = _mem_kv(mem, row(mem_kv_norm_g[l]), w_mem_kv[l].astype(bf))
        x = _mem_attn(x, row(norm_mem_g[l]), w_mem_q[l].astype(bf), k_mem, v_mem,
                      w_mem_o[l].astype(bf))
        assert depth == 1
        x = _ffn(x, row(norm_ffn_g[l]), w_ffn_up[l].astype(bf), ffn_conv_w[l],
                 w_ffn_down[l].astype(bf), row(final_norm_g))
    return x
```

```python
import jax
import jax.numpy as jnp
from jax import lax
from jax.experimental import pallas as pl
from jax.experimental.pallas import tpu as pltpu

HEAD_DIM = 64
ATTN_HEADS = 8
D_ATTN = ATTN_HEADS * HEAD_DIM
MOBA_BLOCK = 256
MOBA_TOPK = 3
MEM_HEADS = 4
EPS = 1e-6

LANES = 128
SUBLANES = 8
VMEM_LIMIT_BYTES = 56 * 1024 * 1024

SEQ_TILE = MOBA_BLOCK
ROW_TILE = 4 * SEQ_TILE
HEAD_PAIR = 2 * HEAD_DIM
MASKED = -1e30
LOG2E = 1.4426950408889634

_NT = (((1,), (1,)), ((), ()))


def _dot(a, b):
    return jnp.dot(a, b, preferred_element_type=jnp.float32)


def _dot_nt(a, b):
    return lax.dot_general(a, b, _NT, preferred_element_type=jnp.float32)


def _rms_scale(x):
    return lax.rsqrt(jnp.mean(x * x, axis=-1, keepdims=True) + EPS)


def _split_bf16(a):
    hi = a.astype(jnp.bfloat16)
    lo = (a - hi.astype(jnp.float32)).astype(jnp.bfloat16)
    return hi, lo


def _causal_conv3(z, tail, w):
    rows = lax.broadcasted_iota(jnp.int32, (SUBLANES, z.shape[1]), 0)
    z1 = pltpu.roll(z, 1, axis=0)
    z2 = pltpu.roll(z, 2, axis=0)
    t1 = pltpu.roll(tail, 1, axis=0)
    t2 = pltpu.roll(tail, 2, axis=0)
    z1 = jnp.concatenate([jnp.where(rows < 1, t1, z1[:SUBLANES]), z1[SUBLANES:]], axis=0)
    z2 = jnp.concatenate([jnp.where(rows < 2, t2, z2[:SUBLANES]), z2[SUBLANES:]], axis=0)
    return z2 * w[0:1] + z1 * w[1:2] + z * w[2:3]


def _resident(a):
    return pl.BlockSpec(a.shape, lambda *_: (0,) * a.ndim, pipeline_mode=pl.Buffered(1))


def _mix_in_kernel(x_ref, g_ref, w_ref, wqvt_ref, cw_ref, cg_ref,
                   qt_ref, k_ref, vt_ref, yc_ref, sel_ref,
                   km_ref, tail_ref):
    s = pl.program_id(1)
    da = D_ATTN
    dc = cw_ref.shape[1]
    nrow = km_ref.shape[0]
    nblk = nrow // ATTN_HEADS
    T = SEQ_TILE
    groups = x_ref.shape[1] // T

    @pl.when(s == 0)
    def _():
        km_ref[...] = jnp.zeros_like(km_ref)
        tail_ref[...] = jnp.zeros_like(tail_ref)

    row_head = lax.broadcasted_iota(jnp.int32, (ATTN_HEADS, da), 0)
    col_head = lax.broadcasted_iota(jnp.int32, (ATTN_HEADS, da), 1) // HEAD_DIM
    tail = tail_ref[...]

    for gi in range(groups):
        blk = s * groups + gi
        rows = slice(gi * T, (gi + 1) * T)
        x = x_ref[0, rows, :]
        h = ((x * _rms_scale(x)) * g_ref[...]).astype(jnp.bfloat16)

        o = 3 * da
        cgate = _dot(h, w_ref[:, o + dc:o + 2 * dc])
        u = _dot(h, w_ref[:, o + 2 * dc:o + 3 * dc])
        z = cgate * u
        conv = _causal_conv3(z, tail, cw_ref[...])
        tail = z[T - SUBLANES:]
        y = _dot(h, w_ref[:, o:o + dc]) * conv
        yc_ref[0, rows, :] = ((y * _rms_scale(y)) * cg_ref[...]).astype(jnp.bfloat16)

        qt = _dot_nt(wqvt_ref[:da, :], h)
        vt_ref[0, gi] = _dot_nt(wqvt_ref[da:, :], h).astype(jnp.bfloat16)
        qt_ref[0, gi] = (qt * (HEAD_DIM ** -0.5 * LOG2E)).astype(jnp.bfloat16)

        qh, ql = _split_bf16(qt)
        kmh, kml = _split_bf16(km_ref[...])
        gate = _dot(jnp.concatenate([kmh, kml], axis=0), qh)
        gate = gate[:nrow] + gate[nrow:] + _dot(kmh, ql)
        g = [jnp.where(n < blk, gate[n * ATTN_HEADS:(n + 1) * ATTN_HEADS], -jnp.inf)
             for n in range(nblk)]
        bias = []
        for n in range(nblk):
            rank = jnp.zeros(g[n].shape, jnp.int32)
            for o in range(nblk):
                if o != n:
                    beats = (g[o] >= g[n]) if o < n else (g[o] > g[n])
                    rank = rank + jnp.where(beats, 1, 0)
            chosen = jnp.logical_and(n < blk, rank < MOBA_TOPK)
            bias.append(jnp.where(chosen, 0.0, MASKED).astype(jnp.float32))
        sel_ref[0, :, rows] = jnp.concatenate(bias, axis=0)

        k = _dot(h, w_ref[:, da:2 * da])
        k_ref[0, rows, :] = k.astype(jnp.bfloat16)

        kmean = jnp.mean(k, axis=0, keepdims=True)
        base = pl.multiple_of(blk * ATTN_HEADS, ATTN_HEADS)
        km_ref[pl.ds(base, ATTN_HEADS), :] = jnp.where(row_head == col_head, kmean, 0.0)

    tail_ref[...] = tail


def _mix_in(x, g, w_in, w_qvt, conv_w, conv_g):
    B, S, D = x.shape
    T = ROW_TILE
    nb = S // SEQ_TILE
    dc = conv_w.shape[1]
    tile = lambda n: pl.BlockSpec((1, T, n), lambda b, s: (b, s, 0))
    blocked = pl.BlockSpec((1, T // SEQ_TILE, D_ATTN, SEQ_TILE), lambda b, s: (b, s, 0, 0))
    bf = jnp.bfloat16
    return pl.pallas_call(
        _mix_in_kernel,
        grid=(B, S // T),
        in_specs=[tile(D), _resident(g), _resident(w_in), _resident(w_qvt),
                  _resident(conv_w), _resident(conv_g)],
        out_specs=[blocked,
                   tile(D_ATTN),
                   blocked,
                   tile(dc),
                   pl.BlockSpec((1, nb * ATTN_HEADS, T), lambda b, s: (b, 0, s))],
        out_shape=[jax.ShapeDtypeStruct((B, nb, D_ATTN, SEQ_TILE), bf),
                   jax.ShapeDtypeStruct((B, S, D_ATTN), bf),
                   jax.ShapeDtypeStruct((B, nb, D_ATTN, SEQ_TILE), bf),
                   jax.ShapeDtypeStruct((B, S, dc), bf),
                   jax.ShapeDtypeStruct((B, nb * ATTN_HEADS, S), jnp.float32)],
        scratch_shapes=[pltpu.VMEM((nb * ATTN_HEADS, D_ATTN), jnp.float32),
                        pltpu.VMEM((SUBLANES, dc), jnp.float32)],
        compiler_params=pltpu.CompilerParams(
            dimension_semantics=("parallel", "arbitrary"),
            vmem_limit_bytes=VMEM_LIMIT_BYTES),
        name="mix_in",
    )(x, g, w_in, w_qvt, conv_w, conv_g)


def _mix_attn_kernel(qt_ref, k_ref, vt_ref, sel_ref, yc_ref, x_ref, ag_ref, w_ref,
                     o_ref, qm_ref, s0_ref, mx0_ref, s1_ref, mx1_ref, m_ref, l_ref, acc_ref):
    j = pl.program_id(1)
    T = SEQ_TILE
    nblk = vt_ref.shape[1]

    zeros = jnp.zeros((HEAD_DIM, T), jnp.bfloat16)
    ones = jnp.ones((2 * SUBLANES, T), jnp.bfloat16)

    def load_queries(tile):
        for h in range(ATTN_HEADS):
            qh = qt_ref[0, tile, h * HEAD_DIM:(h + 1) * HEAD_DIM, :]
            qm_ref[h * HEAD_PAIR:(h + 1) * HEAD_PAIR, :] = jnp.concatenate(
                [qh, zeros] if h % 2 == 0 else [zeros, qh], axis=0)

    key = lax.broadcasted_iota(jnp.int32, (T, T), 0)
    qry = lax.broadcasted_iota(jnp.int32, (T, T), 1)
    causal = key <= qry

    def score_head(h, blk, s_ref, mx_ref, own):
        p = h // 2
        start = pl.multiple_of(blk * T, T)
        kp = k_ref[0, pl.ds(start, T), p * HEAD_PAIR:(p + 1) * HEAD_PAIR]
        sc = _dot(kp, qm_ref[h * HEAD_PAIR:(h + 1) * HEAD_PAIR, :])
        if own:
            sc = jnp.where(causal, sc, MASKED)
        s_ref[h] = sc
        mx = jnp.max(sc.reshape(T // SUBLANES, SUBLANES, T), axis=0)
        for sh in (4, 2, 1):
            mx = jnp.maximum(mx, pltpu.roll(mx, sh, axis=0))
        mx_ref[h] = mx

    def softmax_head(h, visit, s_ref, mx_ref):
        rows = slice(h * HEAD_DIM, (h + 1) * HEAD_DIM)
        blk = jnp.where(visit == 0, j, visit - 1)
        sel_row = jnp.maximum(visit - 1, 0) * ATTN_HEADS + h
        bias = jnp.broadcast_to(sel_ref[0, pl.ds(sel_row, 1), :], (SUBLANES, T))
        bias = jnp.where(visit == 0, 0.0, bias)
        m_old = m_ref[h]
        m_new = jnp.maximum(m_old, mx_ref[h] + bias)
        alpha = jnp.exp2(m_old - m_new)
        shift = m_new - bias
        e = jnp.exp2(s_ref[h].reshape(T // SUBLANES, SUBLANES, T) - shift[None])
        lhs = jnp.concatenate([vt_ref[0, blk, rows, :], ones], axis=0)
        pv = _dot(lhs, e.reshape(T, T).astype(jnp.bfloat16))
        m_ref[h] = m_new
        l_ref[h] = alpha * l_ref[h] + pv[HEAD_DIM:HEAD_DIM + SUBLANES]
        acc3 = acc_ref[rows, :].reshape(HEAD_DIM // SUBLANES, SUBLANES, T)
        acc3 = acc3 * alpha[None] + pv[:HEAD_DIM].reshape(HEAD_DIM // SUBLANES, SUBLANES, T)
        acc_ref[rows, :] = acc3.reshape(HEAD_DIM, T)

    def stage(visit, cur, nxt):
        for h in range(ATTN_HEADS):
            softmax_head(h, visit, *cur)
            if nxt is not None:
                score_head(h, visit, *nxt, False)

    buf0 = (s0_ref, mx0_ref)
    buf1 = (s1_ref, mx1_ref)

    @pl.when(j == 0)
    def _():
        load_queries(0)
        for h in range(ATTN_HEADS):
            score_head(h, 0, *buf0, True)

    m_ref[...] = jnp.full(m_ref.shape, MASKED, jnp.float32)
    l_ref[...] = jnp.zeros_like(l_ref)
    acc_ref[...] = jnp.zeros_like(acc_ref)

    def visit_pair(i, carry):
        stage(2 * i, buf0, buf1)
        stage(2 * i + 1, buf1, buf0)
        return carry

    lax.fori_loop(0, j // 2, visit_pair, 0)

    @pl.when(j % 2 == 1)
    def _():
        stage(j - 1, buf0, buf1)

    def finish(cur):
        nxt = jnp.minimum(j + 1, nblk - 1)
        load_queries(nxt)
        for h in range(ATTN_HEADS):
            softmax_head(h, j, *cur)
            score_head(h, nxt, *buf0, True)
        conv_proj = _dot(yc_ref[0], w_ref[D_ATTN:, :])
        yat = jnp.concatenate(
            [acc_ref[h * HEAD_DIM:(h + 1) * HEAD_DIM, :] / l_ref[h, 0:1, :]
             for h in range(ATTN_HEADS)], axis=0)
        scale = lax.rsqrt(jnp.mean(yat * yat, axis=0, keepdims=True) + EPS)
        yant = ((yat * scale) * ag_ref[...]).astype(jnp.bfloat16)
        proj = lax.dot_general(yant, w_ref[:D_ATTN, :], (((0,), (0,)), ((), ())),
                               preferred_element_type=jnp.float32)
        o_ref[0] = x_ref[0] + conv_proj + proj

    @pl.when(j % 2 == 0)
    def _():
        finish(buf0)

    @pl.when(j % 2 == 1)
    def _():
        finish(buf1)


def _mix_attn(qt, k, vt, sel, yc, x, attn_g, w_out):
    B, S, D = x.shape
    T = SEQ_TILE
    nb = S // T
    tile = lambda n: pl.BlockSpec((1, T, n), lambda b, s: (b, s, 0))
    seq = lambda n: pl.BlockSpec((1, S, n), lambda b, s: (b, 0, 0))
    stat = pltpu.VMEM((ATTN_HEADS, SUBLANES, T), jnp.float32)
    return pl.pallas_call(
        _mix_attn_kernel,
        grid=(B, nb),
        in_specs=[pl.BlockSpec((1, nb, D_ATTN, T), lambda b, s: (b, 0, 0, 0)), seq(D_ATTN),
                  pl.BlockSpec((1, nb, D_ATTN, T), lambda b, s: (b, 0, 0, 0)),
                  pl.BlockSpec((1, nb * ATTN_HEADS, T), lambda b, s: (b, 0, s)),
                  tile(yc.shape[2]), tile(D), _resident(attn_g), _resident(w_out)],
        out_specs=tile(D),
        out_shape=jax.ShapeDtypeStruct((B, S, D), jnp.float32),
        scratch_shapes=[pltpu.VMEM((ATTN_HEADS * HEAD_PAIR, T), jnp.bfloat16),
                        pltpu.VMEM((ATTN_HEADS, T, T), jnp.float32),
                        stat,
                        pltpu.VMEM((ATTN_HEADS, T, T), jnp.float32),
                        stat, stat, stat,
                        pltpu.VMEM((D_ATTN, T), jnp.float32)],
        compiler_params=pltpu.CompilerParams(
            dimension_semantics=("parallel", "arbitrary"),
            vmem_limit_bytes=VMEM_LIMIT_BYTES),
        name="mix_attn",
    )(qt, k, vt, sel, yc, x, attn_g, w_out)


def _mem_kv_kernel(m_ref, g_ref, w_ref, k_ref, v_ref):
    d = m_ref.shape[2]
    m = m_ref[0]
    h = ((m * _rms_scale(m)) * g_ref[...]).astype(jnp.bfloat16)
    k_ref[0] = _dot(h, w_ref[:, :d]).astype(jnp.bfloat16)
    v_ref[0] = _dot(h, w_ref[:, d:]).astype(jnp.bfloat16)


def _mem_kv(mem, g, w_kv):
    B, M, D = mem.shape
    blk = pl.BlockSpec((1, M, D), lambda b: (b, 0, 0))
    return pl.pallas_call(
        _mem_kv_kernel,
        grid=(B,),
        in_specs=[blk, _resident(g), _resident(w_kv)],
        out_specs=[blk, blk],
        out_shape=[jax.ShapeDtypeStruct((B, M, D), jnp.bfloat16)] * 2,
        compiler_params=pltpu.CompilerParams(
            dimension_semantics=("parallel",),
            vmem_limit_bytes=VMEM_LIMIT_BYTES),
        name="mem_kv",
    )(mem, g, w_kv)


def _mem_attn_kernel(x_ref, g_ref, wq_ref, k_ref, v_ref, wo_ref, o_ref):
    d = x_ref.shape[2]
    hd = d // MEM_HEADS
    heads = [slice(i * hd, (i + 1) * hd) for i in range(MEM_HEADS)]

    def scores(r):
        x = x_ref[0, r:r + SEQ_TILE, :]
        h = ((x * _rms_scale(x)) * g_ref[...]).astype(jnp.bfloat16)
        q = (_dot(h, wq_ref[...]) * (hd ** -0.5 * LOG2E)).astype(jnp.bfloat16)
        return [_dot_nt(q[:, c], k_ref[0, :, c]) for c in heads]

    groups = list(range(0, x_ref.shape[1], SEQ_TILE))
    sc_next = scores(groups[0])
    for gi, r in enumerate(groups):
        sc = sc_next
        if gi + 1 < len(groups):
            sc_next = scores(groups[gi + 1])
        outs = []
        for s_h, c in zip(sc, heads):
            e = jnp.exp2(s_h - jnp.max(s_h, axis=-1, keepdims=True))
            o = _dot(e.astype(jnp.bfloat16), v_ref[0, :, c])
            outs.append((o / jnp.sum(e, axis=-1, keepdims=True)).astype(jnp.bfloat16))
        o_ref[0, r:r + SEQ_TILE, :] = (x_ref[0, r:r + SEQ_TILE, :]
                                       + _dot(jnp.concatenate(outs, axis=1), wo_ref[...]))


def _mem_attn(x, g, w_q, k_mem, v_mem, w_o):
    B, S, D = x.shape
    M = k_mem.shape[1]
    T = ROW_TILE
    tile = pl.BlockSpec((1, T, D), lambda b, s: (b, s, 0))
    memb = pl.BlockSpec((1, M, D), lambda b, s: (b, 0, 0))
    return pl.pallas_call(
        _mem_attn_kernel,
        grid=(B, S // T),
        in_specs=[tile, _resident(g), _resident(w_q), memb, memb, _resident(w_o)],
        out_specs=tile,
        out_shape=jax.ShapeDtypeStruct((B, S, D), jnp.float32),
        compiler_params=pltpu.CompilerParams(
            dimension_semantics=("parallel", "parallel"),
            vmem_limit_bytes=VMEM_LIMIT_BYTES),
        name="mem_attn",
    )(x, g, w_q, k_mem, v_mem, w_o)


def _ffn_kernel(x_ref, g_ref, wu_ref, cw_ref, wd_ref, fg_ref, o_ref, tail_ref):
    s = pl.program_id(1)
    dff = cw_ref.shape[1]

    @pl.when(s == 0)
    def _():
        tail_ref[...] = jnp.zeros_like(tail_ref)

    tail = tail_ref[...]
    for r in range(0, x_ref.shape[1], SEQ_TILE):
        x = x_ref[0, r:r + SEQ_TILE, :]
        h = ((x * _rms_scale(x)) * g_ref[...]).astype(jnp.bfloat16)
        pre = _dot(h, wu_ref[:, :dff])
        gate = _causal_conv3(pre, tail, cw_ref[...])
        tail = pre[pre.shape[0] - SUBLANES:]
        u = _dot(h, wu_ref[:, dff:])
        act = (gate * (1.0 / (1.0 + jnp.exp(-gate))) * u).astype(jnp.bfloat16)
        y = x + _dot(act, wd_ref[...])
        o_ref[0, r:r + SEQ_TILE, :] = (y * _rms_scale(y)) * fg_ref[...]
    tail_ref[...] = tail


def _ffn(x, g, w_up, conv_w, w_down, final_g):
    B, S, D = x.shape
    T = ROW_TILE
    dff = conv_w.shape[1]
    tile = pl.BlockSpec((1, T, D), lambda b, s: (b, s, 0))
    return pl.pallas_call(
        _ffn_kernel,
        grid=(B, S // T),
        in_specs=[tile, _resident(g), _resident(w_up), _resident(conv_w),
                  _resident(w_down), _resident(final_g)],
        out_specs=tile,
        out_shape=jax.ShapeDtypeStruct((B, S, D), jnp.float32),
        scratch_shapes=[pltpu.VMEM((SUBLANES, dff), jnp.float32)],
        compiler_params=pltpu.CompilerParams(
            dimension_semantics=("parallel", "arbitrary"),
            vmem_limit_bytes=VMEM_LIMIT_BYTES),
        name="ffn",
    )(x, g, w_up, conv_w, w_down, final_g)


def kernel(x, mem, norm_mix_g, w_in_mix, conv_mix_w, attn_out_g, conv_out_g, w_out_mix, norm_mem_g, mem_kv_norm_g, w_mem_q, w_mem_kv, w_mem_o, norm_ffn_g, w_ffn_up, ffn_conv_w, w_ffn_down, final_norm_g):
    depth = w_in_mix.shape[0]
    bf = jnp.bfloat16
    row = lambda a: a.reshape(1, -1)
    for l in range(depth):
        w_in = w_in_mix[l].astype(bf)
        w_qvt = jnp.concatenate([w_in[:, :D_ATTN], w_in[:, 2 * D_ATTN:3 * D_ATTN]], axis=1).T
        qt, k, vt, yc, sel = _mix_in(x, row(norm_mix_g[l]), w_in, w_qvt,
                                     conv_mix_w[l], row(conv_out_g[l]))
        attn_g_cols = jnp.broadcast_to(attn_out_g[l][:, None], (D_ATTN, SEQ_TILE))
        x = _mix_attn(qt, k, vt, sel, yc, x, attn_g_cols, w_out_mix[l].astype(bf))
        k_mem, v_mem = _mem_kv(mem, row(mem_kv_norm_g[l]), w_mem_kv[l].astype(bf))
        x = _mem_attn(x, row(norm_mem_g[l]), w_mem_q[l].astype(bf), k_mem, v_mem,
                      w_mem_o[l].astype(bf))
        assert depth == 1
        x = _ffn(x, row(norm_ffn_g[l]), w_ffn_up[l].astype(bf), ffn_conv_w[l],
                 w_ffn_down[l].astype(bf), row(final_norm_g))
    return x
```

```python
import functools

import jax
import jax.numpy as jnp
from jax import lax
from jax.experimental import pallas as pl
from jax.experimental.pallas import tpu as pltpu

HEAD_DIM = 64
ATTN_HEADS = 8
D_ATTN = ATTN_HEADS * HEAD_DIM
MOBA_BLOCK = 256
MOBA_TOPK = 3
MEM_HEADS = 4
EPS = 1e-6

LANES = 128
SUBLANES = 8
VMEM_LIMIT_BYTES = 56 * 1024 * 1024

SEQ_TILE = MOBA_BLOCK
ROW_TILE = 4 * SEQ_TILE
HEAD_PAIR = 2 * HEAD_DIM
MASKED = -1e30
LOG2E = 1.4426950408889634

_NT = (((1,), (1,)), ((), ()))


def _dot(a, b):
    return jnp.dot(a, b, preferred_element_type=jnp.float32)


def _dot_nt(a, b):
    return lax.dot_general(a, b, _NT, preferred_element_type=jnp.float32)


def _rms_scale(x):
    return lax.rsqrt(jnp.mean(x * x, axis=-1, keepdims=True) + EPS)


def _split_bf16(a):
    hi = a.astype(jnp.bfloat16)
    lo = (a - hi.astype(jnp.float32)).astype(jnp.bfloat16)
    return hi, lo


def _causal_conv3(z, tail, w):
    rows = lax.broadcasted_iota(jnp.int32, (SUBLANES, z.shape[1]), 0)
    z1 = pltpu.roll(z, 1, axis=0)
    z2 = pltpu.roll(z, 2, axis=0)
    t1 = pltpu.roll(tail, 1, axis=0)
    t2 = pltpu.roll(tail, 2, axis=0)
    z1 = jnp.concatenate([jnp.where(rows < 1, t1, z1[:SUBLANES]), z1[SUBLANES:]], axis=0)
    z2 = jnp.concatenate([jnp.where(rows < 2, t2, z2[:SUBLANES]), z2[SUBLANES:]], axis=0)
    return z2 * w[0:1] + z1 * w[1:2] + z * w[2:3]


def _resident(a):
    return pl.BlockSpec(a.shape, lambda *_: (0,) * a.ndim, pipeline_mode=pl.Buffered(1))


def _mix_in_kernel(x_ref, g_ref, w_ref, wqvt_ref, cw_ref, cg_ref,
                   qt_ref, k_ref, vt_ref, yc_ref, sel_ref,
                   km_ref, tail_ref):
    s = pl.program_id(1)
    da = D_ATTN
    dc = cw_ref.shape[1]
    nrow = km_ref.shape[0]
    nblk = nrow // ATTN_HEADS
    T = SEQ_TILE
    groups = x_ref.shape[1] // T

    @pl.when(s == 0)
    def _():
        km_ref[...] = jnp.zeros_like(km_ref)
        tail_ref[...] = jnp.zeros_like(tail_ref)

    row_head = lax.broadcasted_iota(jnp.int32, (ATTN_HEADS, da), 0)
    col_head = lax.broadcasted_iota(jnp.int32, (ATTN_HEADS, da), 1) // HEAD_DIM
    tail = tail_ref[...]

    for gi in range(groups):
        blk = s * groups + gi
        rows = slice(gi * T, (gi + 1) * T)
        x = x_ref[0, rows, :]
        h = ((x * _rms_scale(x)) * g_ref[...]).astype(jnp.bfloat16)

        o = 3 * da
        cgate = _dot(h, w_ref[:, o + dc:o + 2 * dc])
        u = _dot(h, w_ref[:, o + 2 * dc:o + 3 * dc])
        z = cgate * u
        conv = _causal_conv3(z, tail, cw_ref[...])
        tail = z[T - SUBLANES:]
        y = _dot(h, w_ref[:, o:o + dc]) * conv
        yc_ref[0, rows, :] = ((y * _rms_scale(y)) * cg_ref[...]).astype(jnp.bfloat16)

        qt = _dot_nt(wqvt_ref[:da, :], h)
        vt_ref[0, gi] = _dot_nt(wqvt_ref[da:, :], h).astype(jnp.bfloat16)
        qt_ref[0, gi] = (qt * (HEAD_DIM ** -0.5 * LOG2E)).astype(jnp.bfloat16)

        qh, ql = _split_bf16(qt)
        kmh, kml = _split_bf16(km_ref[...])
        gate = _dot(jnp.concatenate([kmh, kml], axis=0), qh)
        gate = gate[:nrow] + gate[nrow:] + _dot(kmh, ql)
        g = [jnp.where(n < blk, gate[n * ATTN_HEADS:(n + 1) * ATTN_HEADS], -jnp.inf)
             for n in range(nblk)]
        bias = []
        for n in range(nblk):
            rank = jnp.zeros(g[n].shape, jnp.int32)
            for o in range(nblk):
                if o != n:
                    beats = (g[o] >= g[n]) if o < n else (g[o] > g[n])
                    rank = rank + jnp.where(beats, 1, 0)
            chosen = jnp.logical_and(n < blk, rank < MOBA_TOPK)
            bias.append(jnp.where(chosen, 0.0, MASKED).astype(jnp.float32))
        sel_ref[0, :, rows] = jnp.concatenate(bias, axis=0)

        k = _dot(h, w_ref[:, da:2 * da])
        k_ref[0, rows, :] = k.astype(jnp.bfloat16)

        kmean = jnp.mean(k, axis=0, keepdims=True)
        base = pl.multiple_of(blk * ATTN_HEADS, ATTN_HEADS)
        km_ref[pl.ds(base, ATTN_HEADS), :] = jnp.where(row_head == col_head, kmean, 0.0)

    tail_ref[...] = tail


def _mix_in(x, g, w_in, w_qvt, conv_w, conv_g):
    B, S, D = x.shape
    T = 2 * ROW_TILE
    nb = S // SEQ_TILE
    dc = conv_w.shape[1]
    tile = lambda n: pl.BlockSpec((1, T, n), lambda b, s: (b, s, 0))
    blocked = pl.BlockSpec((1, T // SEQ_TILE, D_ATTN, SEQ_TILE), lambda b, s: (b, s, 0, 0))
    bf = jnp.bfloat16
    return pl.pallas_call(
        _mix_in_kernel,
        grid=(B, S // T),
        in_specs=[tile(D), _resident(g), _resident(w_in), _resident(w_qvt),
                  _resident(conv_w), _resident(conv_g)],
        out_specs=[blocked,
                   tile(D_ATTN),
                   blocked,
                   tile(dc),
                   pl.BlockSpec((1, nb * ATTN_HEADS, T), lambda b, s: (b, 0, s))],
        out_shape=[jax.ShapeDtypeStruct((B, nb, D_ATTN, SEQ_TILE), bf),
                   jax.ShapeDtypeStruct((B, S, D_ATTN), bf),
                   jax.ShapeDtypeStruct((B, nb, D_ATTN, SEQ_TILE), bf),
                   jax.ShapeDtypeStruct((B, S, dc), bf),
                   jax.ShapeDtypeStruct((B, nb * ATTN_HEADS, S), jnp.float32)],
        scratch_shapes=[pltpu.VMEM((nb * ATTN_HEADS, D_ATTN), jnp.float32),
                        pltpu.VMEM((SUBLANES, dc), jnp.float32)],
        compiler_params=pltpu.CompilerParams(
            dimension_semantics=("parallel", "arbitrary"),
            vmem_limit_bytes=VMEM_LIMIT_BYTES),
        name="mix_in",
    )(x, g, w_in, w_qvt, conv_w, conv_g)


def _mix_attn_kernel(qt_ref, k_ref, vt_ref, sel_ref, yc_ref, x_ref, ag_ref, w_ref,
                     o_ref, qm_ref, s0_ref, mx0_ref, s1_ref, mx1_ref, m_ref, l_ref, acc_ref):
    j = pl.program_id(1)
    T = SEQ_TILE
    nblk = vt_ref.shape[1]

    zeros = jnp.zeros((HEAD_DIM, T), jnp.bfloat16)
    ones = jnp.ones((2 * SUBLANES, T), jnp.bfloat16)

    def load_queries(tile):
        for h in range(ATTN_HEADS):
            qh = qt_ref[0, tile, h * HEAD_DIM:(h + 1) * HEAD_DIM, :]
            qm_ref[h * HEAD_PAIR:(h + 1) * HEAD_PAIR, :] = jnp.concatenate(
                [qh, zeros] if h % 2 == 0 else [zeros, qh], axis=0)

    key = lax.broadcasted_iota(jnp.int32, (T, T), 0)
    qry = lax.broadcasted_iota(jnp.int32, (T, T), 1)
    causal = key <= qry

    def score_head(h, blk, s_ref, mx_ref, own):
        p = h // 2
        start = pl.multiple_of(blk * T, T)
        kp = k_ref[0, pl.ds(start, T), p * HEAD_PAIR:(p + 1) * HEAD_PAIR]
        sc = _dot(kp, qm_ref[h * HEAD_PAIR:(h + 1) * HEAD_PAIR, :])
        if own:
            sc = jnp.where(causal, sc, MASKED)
        s_ref[h] = sc
        mx = jnp.max(sc.reshape(T // SUBLANES, SUBLANES, T), axis=0)
        for sh in (4, 2, 1):
            mx = jnp.maximum(mx, pltpu.roll(mx, sh, axis=0))
        mx_ref[h] = mx

    def softmax_head(h, visit, s_ref, mx_ref):
        rows = slice(h * HEAD_DIM, (h + 1) * HEAD_DIM)
        blk = jnp.where(visit == 0, j, visit - 1)
        sel_row = jnp.maximum(visit - 1, 0) * ATTN_HEADS + h
        bias = jnp.broadcast_to(sel_ref[0, pl.ds(sel_row, 1), :], (SUBLANES, T))
        bias = jnp.where(visit == 0, 0.0, bias)
        m_old = m_ref[h]
        m_new = jnp.maximum(m_old, mx_ref[h] + bias)
        alpha = jnp.exp2(m_old - m_new)
        shift = m_new - bias
        e = jnp.exp2(s_ref[h].reshape(T // SUBLANES, SUBLANES, T) - shift[None])
        lhs = jnp.concatenate([vt_ref[0, blk, rows, :], ones], axis=0)
        pv = _dot(lhs, e.reshape(T, T).astype(jnp.bfloat16))
        m_ref[h] = m_new
        l_ref[h] = alpha * l_ref[h] + pv[HEAD_DIM:HEAD_DIM + SUBLANES]
        acc3 = acc_ref[rows, :].reshape(HEAD_DIM // SUBLANES, SUBLANES, T)
        acc3 = acc3 * alpha[None] + pv[:HEAD_DIM].reshape(HEAD_DIM // SUBLANES, SUBLANES, T)
        acc_ref[rows, :] = acc3.reshape(HEAD_DIM, T)

    def stage(visit, cur, nxt):
        for h in range(ATTN_HEADS):
            if nxt is not None:
                score_head(h, visit, *nxt, False)
            softmax_head(h, visit, *cur)

    buf0 = (s0_ref, mx0_ref)
    buf1 = (s1_ref, mx1_ref)

    def reset_state():
        m_ref[...] = jnp.full(m_ref.shape, MASKED, jnp.float32)
        l_ref[...] = jnp.zeros_like(l_ref)
        acc_ref[...] = jnp.zeros_like(acc_ref)

    @pl.when(j == 0)
    def _():
        load_queries(0)
        for h in range(ATTN_HEADS):
            score_head(h, 0, *buf0, True)
        reset_state()

    def visit_pair(i, carry):
        stage(2 * i, buf0, buf1)
        stage(2 * i + 1, buf1, buf0)
        return carry

    lax.fori_loop(0, j // 2, visit_pair, 0)

    @pl.when(j % 2 == 1)
    def _():
        stage(j - 1, buf0, buf1)

    def finish(cur):
        nxt = jnp.minimum(j + 1, nblk - 1)
        load_queries(nxt)
        for h in range(ATTN_HEADS):
            softmax_head(h, j, *cur)
            score_head(h, nxt, *buf0, True)
        conv_proj = _dot(yc_ref[0], w_ref[D_ATTN:, :])
        yat = jnp.concatenate(
            [acc_ref[h * HEAD_DIM:(h + 1) * HEAD_DIM, :] / l_ref[h, 0:1, :]
             for h in range(ATTN_HEADS)], axis=0)
        reset_state()
        scale = lax.rsqrt(jnp.mean(yat * yat, axis=0, keepdims=True) + EPS)
        yant = ((yat * scale) * ag_ref[...]).astype(jnp.bfloat16)
        proj = lax.dot_general(yant, w_ref[:D_ATTN, :], (((0,), (0,)), ((), ())),
                               preferred_element_type=jnp.float32)
        o_ref[0] = x_ref[0] + conv_proj + proj

    @pl.when(j % 2 == 0)
    def _():
        finish(buf0)

    @pl.when(j % 2 == 1)
    def _():
        finish(buf1)


def _mix_attn(qt, k, vt, sel, yc, x, attn_g, w_out):
    B, S, D = x.shape
    T = SEQ_TILE
    nb = S // T
    tile = lambda n: pl.BlockSpec((1, T, n), lambda b, s: (b, s, 0))
    seq = lambda n: pl.BlockSpec((1, S, n), lambda b, s: (b, 0, 0))
    stat = pltpu.VMEM((ATTN_HEADS, SUBLANES, T), jnp.float32)
    return pl.pallas_call(
        _mix_attn_kernel,
        grid=(B, nb),
        in_specs=[pl.BlockSpec((1, nb, D_ATTN, T), lambda b, s: (b, 0, 0, 0)), seq(D_ATTN),
                  pl.BlockSpec((1, nb, D_ATTN, T), lambda b, s: (b, 0, 0, 0)),
                  pl.BlockSpec((1, nb * ATTN_HEADS, T), lambda b, s: (b, 0, s)),
                  tile(yc.shape[2]), tile(D), _resident(attn_g), _resident(w_out)],
        out_specs=tile(D),
        out_shape=jax.ShapeDtypeStruct((B, S, D), jnp.float32),
        scratch_shapes=[pltpu.VMEM((ATTN_HEADS * HEAD_PAIR, T), jnp.bfloat16),
                        pltpu.VMEM((ATTN_HEADS, T, T), jnp.float32),
                        stat,
                        pltpu.VMEM((ATTN_HEADS, T, T), jnp.float32),
                        stat, stat, stat,
                        pltpu.VMEM((D_ATTN, T), jnp.float32)],
        compiler_params=pltpu.CompilerParams(
            dimension_semantics=("parallel", "arbitrary"),
            vmem_limit_bytes=VMEM_LIMIT_BYTES),
        name="mix_attn",
    )(qt, k, vt, sel, yc, x, attn_g, w_out)


def _mem_kv_kernel(m_ref, g_ref, w_ref, k_ref, v_ref):
    d = m_ref.shape[2]
    m = m_ref[0]
    h = ((m * _rms_scale(m)) * g_ref[...]).astype(jnp.bfloat16)
    k_ref[0] = _dot(h, w_ref[:, :d]).astype(jnp.bfloat16)
    v_ref[0] = _dot(h, w_ref[:, d:]).astype(jnp.bfloat16)


def _mem_kv(mem, g, w_kv):
    B, M, D = mem.shape
    blk = pl.BlockSpec((1, M, D), lambda b: (b, 0, 0))
    return pl.pallas_call(
        _mem_kv_kernel,
        grid=(B,),
        in_specs=[blk, _resident(g), _resident(w_kv)],
        out_specs=[blk, blk],
        out_shape=[jax.ShapeDtypeStruct((B, M, D), jnp.bfloat16)] * 2,
        compiler_params=pltpu.CompilerParams(
            dimension_semantics=("parallel",),
            vmem_limit_bytes=VMEM_LIMIT_BYTES),
        name="mem_kv",
    )(mem, g, w_kv)


def _mem_attn_kernel(x_ref, g_ref, wq_ref, k_ref, v_ref, wo_ref, o_ref):
    d = x_ref.shape[2]
    hd = d // MEM_HEADS
    heads = [slice(i * hd, (i + 1) * hd) for i in range(MEM_HEADS)]

    def scores(r):
        x = x_ref[0, r:r + SEQ_TILE, :]
        h = ((x * _rms_scale(x)) * g_ref[...]).astype(jnp.bfloat16)
        q = (_dot(h, wq_ref[...]) * (hd ** -0.5 * LOG2E)).astype(jnp.bfloat16)
        return [_dot_nt(q[:, c], k_ref[0, :, c]) for c in heads]

    groups = list(range(0, x_ref.shape[1], SEQ_TILE))
    sc_next = scores(groups[0])
    for gi, r in enumerate(groups):
        sc = sc_next
        if gi + 1 < len(groups):
            sc_next = scores(groups[gi + 1])
        outs = []
        for s_h, c in zip(sc, heads):
            e = jnp.exp2(s_h - jnp.max(s_h, axis=-1, keepdims=True))
            o = _dot(e.astype(jnp.bfloat16), v_ref[0, :, c])
            outs.append((o / jnp.sum(e, axis=-1, keepdims=True)).astype(jnp.bfloat16))
        o_ref[0, r:r + SEQ_TILE, :] = (x_ref[0, r:r + SEQ_TILE, :]
                                       + _dot(jnp.concatenate(outs, axis=1), wo_ref[...]))


def _mem_attn(x, g, w_q, k_mem, v_mem, w_o):
    B, S, D = x.shape
    M = k_mem.shape[1]
    T = 2 * ROW_TILE
    tile = pl.BlockSpec((1, T, D), lambda b, s: (b, s, 0))
    memb = pl.BlockSpec((1, M, D), lambda b, s: (b, 0, 0))
    return pl.pallas_call(
        _mem_attn_kernel,
        grid=(B, S // T),
        in_specs=[tile, _resident(g), _resident(w_q), memb, memb, _resident(w_o)],
        out_specs=tile,
        out_shape=jax.ShapeDtypeStruct((B, S, D), jnp.float32),
        compiler_params=pltpu.CompilerParams(
            dimension_semantics=("parallel", "parallel"),
            vmem_limit_bytes=VMEM_LIMIT_BYTES),
        name="mem_attn",
    )(x, g, w_q, k_mem, v_mem, w_o)


def _ffn_kernel(x_ref, g_ref, wu_ref, cw_ref, wd_ref, fg_ref, o_ref, tail_ref, *, final_norm):
    s = pl.program_id(1)
    dff = cw_ref.shape[1]

    @pl.when(s == 0)
    def _():
        tail_ref[...] = jnp.zeros_like(tail_ref)

    tail = tail_ref[...]
    for r in range(0, x_ref.shape[1], SEQ_TILE):
        x = x_ref[0, r:r + SEQ_TILE, :]
        h = ((x * _rms_scale(x)) * g_ref[...]).astype(jnp.bfloat16)
        pre = _dot(h, wu_ref[:, :dff])
        gate = _causal_conv3(pre, tail, cw_ref[...])
        tail = pre[pre.shape[0] - SUBLANES:]
        u = _dot(h, wu_ref[:, dff:])
        act = (gate * (1.0 / (1.0 + jnp.exp(-gate))) * u).astype(jnp.bfloat16)
        y = x + _dot(act, wd_ref[...])
        if final_norm:
            y = (y * _rms_scale(y)) * fg_ref[...]
        o_ref[0, r:r + SEQ_TILE, :] = y
    tail_ref[...] = tail


def _ffn(x, g, w_up, conv_w, w_down, final_g, final_norm):
    B, S, D = x.shape
    T = ROW_TILE
    dff = conv_w.shape[1]
    tile = pl.BlockSpec((1, T, D), lambda b, s: (b, s, 0))
    return pl.pallas_call(
        functools.partial(_ffn_kernel, final_norm=final_norm),
        grid=(B, S // T),
        in_specs=[tile, _resident(g), _resident(w_up), _resident(conv_w),
                  _resident(w_down), _resident(final_g)],
        out_specs=tile,
        out_shape=jax.ShapeDtypeStruct((B, S, D), jnp.float32),
        scratch_shapes=[pltpu.VMEM((SUBLANES, dff), jnp.float32)],
        compiler_params=pltpu.CompilerParams(
            dimension_semantics=("parallel", "arbitrary"),
            vmem_limit_bytes=VMEM_LIMIT_BYTES),
        name="ffn",
    )(x, g, w_up, conv_w, w_down, final_g)


def kernel(x, mem, norm_mix_g, w_in_mix, conv_mix_w, attn_out_g, conv_out_g, w_out_mix, norm_mem_g, mem_kv_norm_g, w_mem_q, w_mem_kv, w_mem_o, norm_ffn_g, w_ffn_up, ffn_conv_w, w_ffn_down, final_norm_g):
    depth = w_in_mix.shape[0]
    bf = jnp.bfloat16
    row = lambda a: a.reshape(1, -1)
    for l in range(depth):
        w_in = w_in_mix[l].astype(bf)
        w_qvt = jnp.concatenate([w_in[:, :D_ATTN], w_in[:, 2 * D_ATTN:3 * D_ATTN]], axis=1).T
        qt, k, vt, yc, sel = _mix_in(x, row(norm_mix_g[l]), w_in, w_qvt,
                                     conv_mix_w[l], row(conv_out_g[l]))
        attn_g_cols = jnp.broadcast_to(attn_out_g[l][:, None], (D_ATTN, SEQ_TILE))
        x = _mix_attn(qt, k, vt, sel, yc, x, attn_g_cols, w_out_mix[l].astype(bf))
        k_mem, v_mem = _mem_kv(mem, row(mem_kv_norm_g[l]), w_mem_kv[l].astype(bf))
        x = _mem_attn(x, row(norm_mem_g[l]), w_mem_q[l].astype(bf), k_mem, v_mem,
                      w_mem_o[l].astype(bf))
        x = _ffn(x, row(norm_ffn_g[l]), w_ffn_up[l].astype(bf), ffn_conv_w[l],
                 w_ffn_down[l].astype(bf), row(final_norm_g), final_norm=(l == depth - 1))
    return x
```

```python
import functools

import jax
import jax.numpy as jnp
from jax import lax
from jax.experimental import pallas as pl
from jax.experimental.pallas import tpu as pltpu

HEAD_DIM = 64
ATTN_HEADS = 8
D_ATTN = ATTN_HEADS * HEAD_DIM
MOBA_BLOCK = 256
MOBA_TOPK = 3
MEM_HEADS = 4
EPS = 1e-6

LANES = 128
SUBLANES = 8
VMEM_LIMIT_BYTES = 56 * 1024 * 1024

SEQ_TILE = MOBA_BLOCK
ROW_TILE = 4 * SEQ_TILE
HEAD_PAIR = 2 * HEAD_DIM
MASKED = -1e30
LOG2E = 1.4426950408889634

_NT = (((1,), (1,)), ((), ()))


def _dot(a, b):
    return jnp.dot(a, b, preferred_element_type=jnp.float32)


def _dot_nt(a, b):
    return lax.dot_general(a, b, _NT, preferred_element_type=jnp.float32)


def _rms_scale(x):
    return lax.rsqrt(jnp.mean(x * x, axis=-1, keepdims=True) + EPS)


def _split_bf16(a):
    hi = a.astype(jnp.bfloat16)
    lo = (a - hi.astype(jnp.float32)).astype(jnp.bfloat16)
    return hi, lo


def _causal_conv3(z, tail, w):
    rows = lax.broadcasted_iota(jnp.int32, (SUBLANES, z.shape[1]), 0)
    z1 = pltpu.roll(z, 1, axis=0)
    z2 = pltpu.roll(z, 2, axis=0)
    t1 = pltpu.roll(tail, 1, axis=0)
    t2 = pltpu.roll(tail, 2, axis=0)
    z1 = jnp.concatenate([jnp.where(rows < 1, t1, z1[:SUBLANES]), z1[SUBLANES:]], axis=0)
    z2 = jnp.concatenate([jnp.where(rows < 2, t2, z2[:SUBLANES]), z2[SUBLANES:]], axis=0)
    return z2 * w[0:1] + z1 * w[1:2] + z * w[2:3]


def _resident(a):
    return pl.BlockSpec(a.shape, lambda *_: (0,) * a.ndim, pipeline_mode=pl.Buffered(1))


def _mix_in_kernel(x_ref, g_ref, w_ref, wqvt_ref, cw_ref, cg_ref,
                   qt_ref, k_ref, vt_ref, yc_ref, sel_ref,
                   km_ref, tail_ref):
    s = pl.program_id(1)
    da = D_ATTN
    dc = cw_ref.shape[1]
    nrow = km_ref.shape[0]
    nblk = nrow // ATTN_HEADS
    T = SEQ_TILE
    groups = x_ref.shape[1] // T

    @pl.when(s == 0)
    def _():
        km_ref[...] = jnp.zeros_like(km_ref)
        tail_ref[...] = jnp.zeros_like(tail_ref)

    row_head = lax.broadcasted_iota(jnp.int32, (ATTN_HEADS, da), 0)
    col_head = lax.broadcasted_iota(jnp.int32, (ATTN_HEADS, da), 1) // HEAD_DIM
    tail = tail_ref[...]

    for gi in range(groups):
        blk = s * groups + gi
        rows = slice(gi * T, (gi + 1) * T)
        x = x_ref[0, rows, :]
        h = ((x * _rms_scale(x)) * g_ref[...]).astype(jnp.bfloat16)

        o = 3 * da
        cgate = _dot(h, w_ref[:, o + dc:o + 2 * dc])
        u = _dot(h, w_ref[:, o + 2 * dc:o + 3 * dc])
        z = cgate * u
        conv = _causal_conv3(z, tail, cw_ref[...])
        tail = z[T - SUBLANES:]
        y = _dot(h, w_ref[:, o:o + dc]) * conv
        yc_ref[0, rows, :] = ((y * _rms_scale(y)) * cg_ref[...]).astype(jnp.bfloat16)

        qt = _dot_nt(wqvt_ref[:da, :], h)
        vt_ref[0, gi] = _dot_nt(wqvt_ref[da:, :], h).astype(jnp.bfloat16)
        qt_ref[0, gi] = (qt * (HEAD_DIM ** -0.5 * LOG2E)).astype(jnp.bfloat16)

        qh, ql = _split_bf16(qt)
        kmh, kml = _split_bf16(km_ref[...])
        gate = _dot(jnp.concatenate([kmh, kml], axis=0), qh)
        gate = gate[:nrow] + gate[nrow:] + _dot(kmh, ql)
        g = [jnp.where(n < blk, gate[n * ATTN_HEADS:(n + 1) * ATTN_HEADS], -jnp.inf)
             for n in range(nblk)]
        bias = []
        for n in range(nblk):
            rank = jnp.zeros(g[n].shape, jnp.int32)
            for o in range(nblk):
                if o != n:
                    beats = (g[o] >= g[n]) if o < n else (g[o] > g[n])
                    rank = rank + jnp.where(beats, 1, 0)
            chosen = jnp.logical_and(n < blk, rank < MOBA_TOPK)
            bias.append(jnp.where(chosen, 0.0, MASKED).astype(jnp.float32))
        sel_ref[0, :, rows] = jnp.concatenate(bias, axis=0)

        k = _dot(h, w_ref[:, da:2 * da])
        k_ref[0, rows, :] = k.astype(jnp.bfloat16)

        kmean = jnp.mean(k, axis=0, keepdims=True)
        base = pl.multiple_of(blk * ATTN_HEADS, ATTN_HEADS)
        km_ref[pl.ds(base, ATTN_HEADS), :] = jnp.where(row_head == col_head, kmean, 0.0)

    tail_ref[...] = tail


def _mix_in(x, g, w_in, w_qvt, conv_w, conv_g):
    B, S, D = x.shape
    T = 2 * ROW_TILE
    nb = S // SEQ_TILE
    dc = conv_w.shape[1]
    tile = lambda n: pl.BlockSpec((1, T, n), lambda b, s: (b, s, 0))
    blocked = pl.BlockSpec((1, T // SEQ_TILE, D_ATTN, SEQ_TILE), lambda b, s: (b, s, 0, 0))
    bf = jnp.bfloat16
    return pl.pallas_call(
        _mix_in_kernel,
        grid=(B, S // T),
        in_specs=[tile(D), _resident(g), _resident(w_in), _resident(w_qvt),
                  _resident(conv_w), _resident(conv_g)],
        out_specs=[blocked,
                   tile(D_ATTN),
                   blocked,
                   tile(dc),
                   pl.BlockSpec((1, nb * ATTN_HEADS, T), lambda b, s: (b, 0, s))],
        out_shape=[jax.ShapeDtypeStruct((B, nb, D_ATTN, SEQ_TILE), bf),
                   jax.ShapeDtypeStruct((B, S, D_ATTN), bf),
                   jax.ShapeDtypeStruct((B, nb, D_ATTN, SEQ_TILE), bf),
                   jax.ShapeDtypeStruct((B, S, dc), bf),
                   jax.ShapeDtypeStruct((B, nb * ATTN_HEADS, S), jnp.float32)],
        scratch_shapes=[pltpu.VMEM((nb * ATTN_HEADS, D_ATTN), jnp.float32),
                        pltpu.VMEM((SUBLANES, dc), jnp.float32)],
        compiler_params=pltpu.CompilerParams(
            dimension_semantics=("parallel", "arbitrary"),
            vmem_limit_bytes=VMEM_LIMIT_BYTES),
        name="mix_in",
    )(x, g, w_in, w_qvt, conv_w, conv_g)


def _mix_attn_kernel(qt_ref, k_ref, vt_ref, sel_ref, yc_ref, x_ref, ag_ref, w_ref,
                     o_ref, qm_ref, s0_ref, mx0_ref, s1_ref, mx1_ref, m_ref, l_ref, acc_ref):
    j = pl.program_id(1)
    T = SEQ_TILE
    nblk = vt_ref.shape[1]

    zeros = jnp.zeros((HEAD_DIM, T), jnp.bfloat16)
    ones = jnp.ones((2 * SUBLANES, T), jnp.bfloat16)

    def load_queries(tile):
        for h in range(ATTN_HEADS):
            qh = qt_ref[0, tile, h * HEAD_DIM:(h + 1) * HEAD_DIM, :]
            qm_ref[h * HEAD_PAIR:(h + 1) * HEAD_PAIR, :] = jnp.concatenate(
                [qh, zeros] if h % 2 == 0 else [zeros, qh], axis=0)

    key = lax.broadcasted_iota(jnp.int32, (T, T), 0)
    qry = lax.broadcasted_iota(jnp.int32, (T, T), 1)
    causal = key <= qry

    def score_head(h, blk, s_ref, mx_ref, own):
        p = h // 2
        start = pl.multiple_of(blk * T, T)
        kp = k_ref[0, pl.ds(start, T), p * HEAD_PAIR:(p + 1) * HEAD_PAIR]
        sc = _dot(kp, qm_ref[h * HEAD_PAIR:(h + 1) * HEAD_PAIR, :])
        if own:
            sc = jnp.where(causal, sc, MASKED)
        s_ref[h] = sc
        mx = jnp.max(sc.reshape(T // SUBLANES, SUBLANES, T), axis=0)
        for sh in (4, 2, 1):
            mx = jnp.maximum(mx, pltpu.roll(mx, sh, axis=0))
        mx_ref[h] = mx

    def softmax_head(h, visit, s_ref, mx_ref):
        rows = slice(h * HEAD_DIM, (h + 1) * HEAD_DIM)
        blk = jnp.where(visit == 0, j, visit - 1)
        sel_row = jnp.maximum(visit - 1, 0) * ATTN_HEADS + h
        bias = jnp.broadcast_to(sel_ref[0, pl.ds(sel_row, 1), :], (SUBLANES, T))
        bias = jnp.where(visit == 0, 0.0, bias)
        m_old = m_ref[h]
        m_new = jnp.maximum(m_old, mx_ref[h] + bias)
        alpha = jnp.exp2(m_old - m_new)
        shift = m_new - bias
        e = jnp.exp2(s_ref[h].reshape(T // SUBLANES, SUBLANES, T) - shift[None])
        lhs = jnp.concatenate([vt_ref[0, blk, rows, :], ones], axis=0)
        pv = _dot(lhs, e.reshape(T, T).astype(jnp.bfloat16))
        m_ref[h] = m_new
        l_ref[h] = alpha * l_ref[h] + pv[HEAD_DIM:HEAD_DIM + SUBLANES]
        acc3 = acc_ref[rows, :].reshape(HEAD_DIM // SUBLANES, SUBLANES, T)
        acc3 = acc3 * alpha[None] + pv[:HEAD_DIM].reshape(HEAD_DIM // SUBLANES, SUBLANES, T)
        acc_ref[rows, :] = acc3.reshape(HEAD_DIM, T)

    def stage(visit, cur, nxt):
        for h in range(ATTN_HEADS):
            softmax_head(h, visit, *cur)
            if nxt is not None:
                score_head(h, visit, *nxt, False)

    buf0 = (s0_ref, mx0_ref)
    buf1 = (s1_ref, mx1_ref)

    @pl.when(j == 0)
    def _():
        load_queries(0)
        for h in range(ATTN_HEADS):
            score_head(h, 0, *buf0, True)

    m_ref[...] = jnp.full(m_ref.shape, MASKED, jnp.float32)
    l_ref[...] = jnp.zeros_like(l_ref)
    acc_ref[...] = jnp.zeros_like(acc_ref)

    def visit_pair(i, carry):
        stage(2 * i, buf0, buf1)
        stage(2 * i + 1, buf1, buf0)
        return carry

    lax.fori_loop(0, j // 2, visit_pair, 0)

    @pl.when(j % 2 == 1)
    def _():
        stage(j - 1, buf0, buf1)

    def finish(cur):
        nxt = jnp.minimum(j + 1, nblk - 1)
        load_queries(nxt)
        for h in range(ATTN_HEADS):
            softmax_head(h, j, *cur)
            score_head(h, nxt, *buf0, True)
        conv_proj = _dot(yc_ref[0], w_ref[D_ATTN:, :])
        yat = jnp.concatenate(
            [acc_ref[h * HEAD_DIM:(h + 1) * HEAD_DIM, :] / l_ref[h, 0:1, :]
             for h in range(ATTN_HEADS)], axis=0)
        scale = lax.rsqrt(jnp.mean(yat * yat, axis=0, keepdims=True) + EPS)
        yant = ((yat * scale) * ag_ref[...]).astype(jnp.bfloat16)
        proj = lax.dot_general(yant, w_ref[:D_ATTN, :], (((0,), (0,)), ((), ())),
                               preferred_element_type=jnp.float32)
        o_ref[0] = x_ref[0] + conv_proj + proj

    @pl.when(j % 2 == 0)
    def _():
        finish(buf0)

    @pl.when(j % 2 == 1)
    def _():
        finish(buf1)


def _mix_attn(qt, k, vt, sel, yc, x, attn_g, w_out):
    B, S, D = x.shape
    T = SEQ_TILE
    nb = S // T
    tile = lambda n: pl.BlockSpec((1, T, n), lambda b, s: (b, s, 0))
    seq = lambda n: pl.BlockSpec((1, S, n), lambda b, s: (b, 0, 0))
    stat = pltpu.VMEM((ATTN_HEADS, SUBLANES, T), jnp.float32)
    return pl.pallas_call(
        _mix_attn_kernel,
        grid=(B, nb),
        in_specs=[pl.BlockSpec((1, nb, D_ATTN, T), lambda b, s: (b, 0, 0, 0)), seq(D_ATTN),
                  pl.BlockSpec((1, nb, D_ATTN, T), lambda b, s: (b, 0, 0, 0)),
                  pl.BlockSpec((1, nb * ATTN_HEADS, T), lambda b, s: (b, 0, s)),
                  tile(yc.shape[2]), tile(D), _resident(attn_g), _resident(w_out)],
        out_specs=tile(D),
        out_shape=jax.ShapeDtypeStruct((B, S, D), jnp.float32),
        scratch_shapes=[pltpu.VMEM((ATTN_HEADS * HEAD_PAIR, T), jnp.bfloat16),
                        pltpu.VMEM((ATTN_HEADS, T, T), jnp.float32),
                        stat,
                        pltpu.VMEM((ATTN_HEADS, T, T), jnp.float32),
                        stat, stat, stat,
                        pltpu.VMEM((D_ATTN, T), jnp.float32)],
        compiler_params=pltpu.CompilerParams(
            dimension_semantics=("parallel", "arbitrary"),
            vmem_limit_bytes=VMEM_LIMIT_BYTES),
        name="mix_attn",
    )(qt, k, vt, sel, yc, x, attn_g, w_out)


def _mem_kv_kernel(m_ref, g_ref, w_ref, k_ref, v_ref):
    d = m_ref.shape[2]
    m = m_ref[0]
    h = ((m * _rms_scale(m)) * g_ref[...]).astype(jnp.bfloat16)
    k_ref[0] = _dot(h, w_ref[:, :d]).astype(jnp.bfloat16)
    v_ref[0] = _dot(h, w_ref[:, d:]).astype(jnp.bfloat16)


def _mem_kv(mem, g, w_kv):
    B, M, D = mem.shape
    blk = pl.BlockSpec((1, M, D), lambda b: (b, 0, 0))
    return pl.pallas_call(
        _mem_kv_kernel,
        grid=(B,),
        in_specs=[blk, _resident(g), _resident(w_kv)],
        out_specs=[blk, blk],
        out_shape=[jax.ShapeDtypeStruct((B, M, D), jnp.bfloat16)] * 2,
        compiler_params=pltpu.CompilerParams(
            dimension_semantics=("parallel",),
            vmem_limit_bytes=VMEM_LIMIT_BYTES),
        name="mem_kv",
    )(mem, g, w_kv)


def _mem_attn_kernel(x_ref, g_ref, wq_ref, k_ref, v_ref, wo_ref, o_ref):
    d = x_ref.shape[2]
    hd = d // MEM_HEADS
    heads = [slice(i * hd, (i + 1) * hd) for i in range(MEM_HEADS)]

    def scores(r):
        x = x_ref[0, r:r + SEQ_TILE, :]
        h = ((x * _rms_scale(x)) * g_ref[...]).astype(jnp.bfloat16)
        q = (_dot(h, wq_ref[...]) * (hd ** -0.5 * LOG2E)).astype(jnp.bfloat16)
        return [_dot_nt(q[:, c], k_ref[0, :, c]) for c in heads]

    groups = list(range(0, x_ref.shape[1], SEQ_TILE))
    sc_next = scores(groups[0])
    for gi, r in enumerate(groups):
        sc = sc_next
        if gi + 1 < len(groups):
            sc_next = scores(groups[gi + 1])
        outs = []
        for s_h, c in zip(sc, heads):
            e = jnp.exp2(s_h - jnp.max(s_h, axis=-1, keepdims=True))
            o = _dot(e.astype(jnp.bfloat16), v_ref[0, :, c])
            outs.append((o / jnp.sum(e, axis=-1, keepdims=True)).astype(jnp.bfloat16))
        o_ref[0, r:r + SEQ_TILE, :] = (x_ref[0, r:r + SEQ_TILE, :]
                                       + _dot(jnp.concatenate(outs, axis=1), wo_ref[...]))


def _mem_attn(x, g, w_q, k_mem, v_mem, w_o):
    B, S, D = x.shape
    M = k_mem.shape[1]
    T = 2 * ROW_TILE
    tile = pl.BlockSpec((1, T, D), lambda b, s: (b, s, 0))
    memb = pl.BlockSpec((1, M, D), lambda b, s: (b, 0, 0))
    return pl.pallas_call(
        _mem_attn_kernel,
        grid=(B, S // T),
        in_specs=[tile, _resident(g), _resident(w_q), memb, memb, _resident(w_o)],
        out_specs=tile,
        out_shape=jax.ShapeDtypeStruct((B, S, D), jnp.float32),
        compiler_params=pltpu.CompilerParams(
            dimension_semantics=("parallel", "parallel"),
            vmem_limit_bytes=VMEM_LIMIT_BYTES),
        name="mem_attn",
    )(x, g, w_q, k_mem, v_mem, w_o)


def _ffn_kernel(x_ref, g_ref, wu_ref, cw_ref, wd_ref, fg_ref, o_ref, tail_ref, *, final_norm):
    s = pl.program_id(1)
    dff = cw_ref.shape[1]

    @pl.when(s == 0)
    def _():
        tail_ref[...] = jnp.zeros_like(tail_ref)

    tail = tail_ref[...]
    for r in range(0, x_ref.shape[1], SEQ_TILE):
        x = x_ref[0, r:r + SEQ_TILE, :]
        h = ((x * _rms_scale(x)) * g_ref[...]).astype(jnp.bfloat16)
        pre = _dot(h, wu_ref[:, :dff])
        gate = _causal_conv3(pre, tail, cw_ref[...])
        tail = pre[pre.shape[0] - SUBLANES:]
        u = _dot(h, wu_ref[:, dff:])
        act = (gate * (1.0 / (1.0 + jnp.exp(-gate))) * u).astype(jnp.bfloat16)
        y = x + _dot(act, wd_ref[...])
        if final_norm:
            y = (y * _rms_scale(y)) * fg_ref[...]
        o_ref[0, r:r + SEQ_TILE, :] = y
    tail_ref[...] = tail


def _ffn(x, g, w_up, conv_w, w_down, final_g, final_norm):
    B, S, D = x.shape
    T = ROW_TILE
    dff = conv_w.shape[1]
    tile = pl.BlockSpec((1, T, D), lambda b, s: (b, s, 0))
    return pl.pallas_call(
        functools.partial(_ffn_kernel, final_norm=final_norm),
        grid=(B, S // T),
        in_specs=[tile, _resident(g), _resident(w_up), _resident(conv_w),
                  _resident(w_down), _resident(final_g)],
        out_specs=tile,
        out_shape=jax.ShapeDtypeStruct((B, S, D), jnp.float32),
        scratch_shapes=[pltpu.VMEM((SUBLANES, dff), jnp.float32)],
        compiler_params=pltpu.CompilerParams(
            dimension_semantics=("parallel", "arbitrary"),
            vmem_limit_bytes=VMEM_LIMIT_BYTES),
        name="ffn",
    )(x, g, w_up, conv_w, w_down, final_g)


def kernel(x, mem, norm_mix_g, w_in_mix, conv_mix_w, attn_out_g, conv_out_g, w_out_mix, norm_mem_g, mem_kv_norm_g, w_mem_q, w_mem_kv, w_mem_o, norm_ffn_g, w_ffn_up, ffn_conv_w, w_ffn_down, final_norm_g):
    depth = w_in_mix.shape[0]
    bf = jnp.bfloat16
    row = lambda a: a.reshape(1, -1)
    for l in range(depth):
        w_in = w_in_mix[l].astype(bf)
        w_qvt = jnp.concatenate([w_in[:, :D_ATTN], w_in[:, 2 * D_ATTN:3 * D_ATTN]], axis=1).T
        qt, k, vt, yc, sel = _mix_in(x, row(norm_mix_g[l]), w_in, w_qvt,
                                     conv_mix_w[l], row(conv_out_g[l]))
        attn_g_cols = jnp.broadcast_to(attn_out_g[l][:, None], (D_ATTN, SEQ_TILE))
        x = _mix_attn(qt, k, vt, sel, yc, x, attn_g_cols, w_out_mix[l].astype(bf))
        k_mem, v_mem = _mem_kv(mem, row(mem_kv_norm_g[l]), w_mem_kv[l].astype(bf))
        x = _mem_attn(x, row(norm_mem_g[l]), w_mem_q[l].astype(bf), k_mem, v_mem,
                      w_mem_o[l].astype(bf))
        x = _ffn(x, row(norm_ffn_g[l]), w_ffn_up[l].astype(bf), ffn_conv_w[l],
                 w_ffn_down[l].astype(bf), row(final_norm_g), final_norm=(l == depth - 1))
    return x
```

```python
import functools

import jax
import jax.numpy as jnp
from jax import lax
from jax.experimental import pallas as pl
from jax.experimental.pallas import tpu as pltpu

HEAD_DIM = 64
ATTN_HEADS = 8
D_ATTN = ATTN_HEADS * HEAD_DIM
MOBA_BLOCK = 256
MOBA_TOPK = 3
MEM_HEADS = 4
EPS = 1e-6

LANES = 128
SUBLANES = 8
VMEM_LIMIT_BYTES = 56 * 1024 * 1024

SEQ_TILE = MOBA_BLOCK
ROW_TILE = 4 * SEQ_TILE
HEAD_PAIR = 2 * HEAD_DIM
MASKED = -1e30
LOG2E = 1.4426950408889634

_NT = (((1,), (1,)), ((), ()))


def _dot(a, b):
    return jnp.dot(a, b, preferred_element_type=jnp.float32)


def _dot_nt(a, b):
    return lax.dot_general(a, b, _NT, preferred_element_type=jnp.float32)


def _rms_scale(x):
    return lax.rsqrt(jnp.mean(x * x, axis=-1, keepdims=True) + EPS)


def _split_bf16(a):
    hi = a.astype(jnp.bfloat16)
    lo = (a - hi.astype(jnp.float32)).astype(jnp.bfloat16)
    return hi, lo


def _causal_conv3(z, tail, w):
    rows = lax.broadcasted_iota(jnp.int32, (SUBLANES, z.shape[1]), 0)
    z1 = pltpu.roll(z, 1, axis=0)
    z2 = pltpu.roll(z, 2, axis=0)
    t1 = pltpu.roll(tail, 1, axis=0)
    t2 = pltpu.roll(tail, 2, axis=0)
    z1 = jnp.concatenate([jnp.where(rows < 1, t1, z1[:SUBLANES]), z1[SUBLANES:]], axis=0)
    z2 = jnp.concatenate([jnp.where(rows < 2, t2, z2[:SUBLANES]), z2[SUBLANES:]], axis=0)
    return z2 * w[0:1] + z1 * w[1:2] + z * w[2:3]


def _resident(a):
    return pl.BlockSpec(a.shape, lambda *_: (0,) * a.ndim, pipeline_mode=pl.Buffered(1))


def _mix_in_kernel(x_ref, g_ref, w_ref, wqvt_ref, cw_ref, cg_ref,
                   qt_ref, k_ref, vt_ref, yc_ref, sel_ref,
                   km_ref, tail_ref):
    s = pl.program_id(1)
    da = D_ATTN
    dc = cw_ref.shape[1]
    nrow = km_ref.shape[0]
    nblk = nrow // ATTN_HEADS
    T = SEQ_TILE
    groups = x_ref.shape[1] // T

    @pl.when(s == 0)
    def _():
        km_ref[...] = jnp.zeros_like(km_ref)
        tail_ref[...] = jnp.zeros_like(tail_ref)

    row_head = lax.broadcasted_iota(jnp.int32, (ATTN_HEADS, da), 0)
    col_head = lax.broadcasted_iota(jnp.int32, (ATTN_HEADS, da), 1) // HEAD_DIM
    tail = tail_ref[...]

    for gi in range(groups):
        blk = s * groups + gi
        rows = slice(gi * T, (gi + 1) * T)
        x = x_ref[0, rows, :]
        h = ((x * _rms_scale(x)) * g_ref[...]).astype(jnp.bfloat16)

        o = 3 * da
        cgate = _dot(h, w_ref[:, o + dc:o + 2 * dc])
        u = _dot(h, w_ref[:, o + 2 * dc:o + 3 * dc])
        z = cgate * u
        conv = _causal_conv3(z, tail, cw_ref[...])
        tail = z[T - SUBLANES:]
        y = _dot(h, w_ref[:, o:o + dc]) * conv
        yc_ref[0, rows, :] = ((y * _rms_scale(y)) * cg_ref[...]).astype(jnp.bfloat16)

        qt = _dot_nt(wqvt_ref[:da, :], h)
        vt_ref[0, gi] = _dot_nt(wqvt_ref[da:, :], h).astype(jnp.bfloat16)
        qt_ref[0, gi] = (qt * (HEAD_DIM ** -0.5 * LOG2E)).astype(jnp.bfloat16)

        qh, ql = _split_bf16(qt)
        kmh, kml = _split_bf16(km_ref[...])
        gate = _dot(jnp.concatenate([kmh, kml], axis=0), qh)
        gate = gate[:nrow] + gate[nrow:] + _dot(kmh, ql)
        g = [jnp.where(n < blk, gate[n * ATTN_HEADS:(n + 1) * ATTN_HEADS], -jnp.inf)
             for n in range(nblk)]
        bias = []
        for n in range(nblk):
            rank = jnp.zeros(g[n].shape, jnp.int32)
            for o in range(nblk):
                if o != n:
                    beats = (g[o] >= g[n]) if o < n else (g[o] > g[n])
                    rank = rank + jnp.where(beats, 1, 0)
            chosen = jnp.logical_and(n < blk, rank < MOBA_TOPK)
            bias.append(jnp.where(chosen, 0.0, MASKED).astype(jnp.float32))
        sel_ref[0, :, rows] = jnp.concatenate(bias, axis=0)

        k = _dot(h, w_ref[:, da:2 * da])
        k_ref[0, rows, :] = k.astype(jnp.bfloat16)

        kmean = jnp.mean(k, axis=0, keepdims=True)
        base = pl.multiple_of(blk * ATTN_HEADS, ATTN_HEADS)
        km_ref[pl.ds(base, ATTN_HEADS), :] = jnp.where(row_head == col_head, kmean, 0.0)

    tail_ref[...] = tail


def _mix_in(x, g, w_in, w_qvt, conv_w, conv_g):
    B, S, D = x.shape
    T = 2 * ROW_TILE
    nb = S // SEQ_TILE
    dc = conv_w.shape[1]
    tile = lambda n: pl.BlockSpec((1, T, n), lambda b, s: (b, s, 0))
    blocked = pl.BlockSpec((1, T // SEQ_TILE, D_ATTN, SEQ_TILE), lambda b, s: (b, s, 0, 0))
    bf = jnp.bfloat16
    return pl.pallas_call(
        _mix_in_kernel,
        grid=(B, S // T),
        in_specs=[tile(D), _resident(g), _resident(w_in), _resident(w_qvt),
                  _resident(conv_w), _resident(conv_g)],
        out_specs=[blocked,
                   tile(D_ATTN),
                   blocked,
                   tile(dc),
                   pl.BlockSpec((1, nb * ATTN_HEADS, T), lambda b, s: (b, 0, s))],
        out_shape=[jax.ShapeDtypeStruct((B, nb, D_ATTN, SEQ_TILE), bf),
                   jax.ShapeDtypeStruct((B, S, D_ATTN), bf),
                   jax.ShapeDtypeStruct((B, nb, D_ATTN, SEQ_TILE), bf),
                   jax.ShapeDtypeStruct((B, S, dc), bf),
                   jax.ShapeDtypeStruct((B, nb * ATTN_HEADS, S), jnp.float32)],
        scratch_shapes=[pltpu.VMEM((nb * ATTN_HEADS, D_ATTN), jnp.float32),
                        pltpu.VMEM((SUBLANES, dc), jnp.float32)],
        compiler_params=pltpu.CompilerParams(
            dimension_semantics=("parallel", "arbitrary"),
            vmem_limit_bytes=VMEM_LIMIT_BYTES),
        name="mix_in",
    )(x, g, w_in, w_qvt, conv_w, conv_g)


def _mix_attn_kernel(qt_ref, k_ref, vt_ref, sel_ref, yc_ref, x_ref, ag_ref, w_ref,
                     o_ref, qm_ref, s0_ref, mx0_ref, s1_ref, mx1_ref, m_ref, l_ref, acc_ref):
    j = pl.program_id(1)
    T = SEQ_TILE
    nblk = vt_ref.shape[1]

    zeros = jnp.zeros((HEAD_DIM, T), jnp.bfloat16)
    ones = jnp.ones((2 * SUBLANES, T), jnp.bfloat16)

    def load_queries(tile):
        for h in range(ATTN_HEADS):
            qh = qt_ref[0, tile, h * HEAD_DIM:(h + 1) * HEAD_DIM, :]
            qm_ref[h * HEAD_PAIR:(h + 1) * HEAD_PAIR, :] = jnp.concatenate(
                [qh, zeros] if h % 2 == 0 else [zeros, qh], axis=0)

    key = lax.broadcasted_iota(jnp.int32, (T, T), 0)
    qry = lax.broadcasted_iota(jnp.int32, (T, T), 1)
    causal = key <= qry

    def score_head(h, blk, s_ref, mx_ref, own):
        p = h // 2
        start = pl.multiple_of(blk * T, T)
        kp = k_ref[0, pl.ds(start, T), p * HEAD_PAIR:(p + 1) * HEAD_PAIR]
        sc = _dot(kp, qm_ref[h * HEAD_PAIR:(h + 1) * HEAD_PAIR, :])
        if own:
            sc = jnp.where(causal, sc, MASKED)
        s_ref[h] = sc
        mx = jnp.max(sc.reshape(T // SUBLANES, SUBLANES, T), axis=0)
        for sh in (4, 2, 1):
            mx = jnp.maximum(mx, pltpu.roll(mx, sh, axis=0))
        mx_ref[h] = mx

    def softmax_head(h, visit, s_ref, mx_ref):
        rows = slice(h * HEAD_DIM, (h + 1) * HEAD_DIM)
        blk = jnp.where(visit == 0, j, visit - 1)
        sel_row = jnp.maximum(visit - 1, 0) * ATTN_HEADS + h
        bias = jnp.broadcast_to(sel_ref[0, pl.ds(sel_row, 1), :], (SUBLANES, T))
        bias = jnp.where(visit == 0, 0.0, bias)
        m_old = m_ref[h]
        m_new = jnp.maximum(m_old, mx_ref[h] + bias)
        alpha = jnp.exp2(m_old - m_new)
        shift = m_new - bias
        e = jnp.exp2(s_ref[h].reshape(T // SUBLANES, SUBLANES, T) - shift[None])
        lhs = jnp.concatenate([vt_ref[0, blk, rows, :], ones], axis=0)
        pv = _dot(lhs, e.reshape(T, T).astype(jnp.bfloat16))
        m_ref[h] = m_new
        l_ref[h] = alpha * l_ref[h] + pv[HEAD_DIM:HEAD_DIM + SUBLANES]
        acc3 = acc_ref[rows, :].reshape(HEAD_DIM // SUBLANES, SUBLANES, T)
        acc3 = acc3 * alpha[None] + pv[:HEAD_DIM].reshape(HEAD_DIM // SUBLANES, SUBLANES, T)
        acc_ref[rows, :] = acc3.reshape(HEAD_DIM, T)

    def stage(visit, cur, nxt):
        for h in range(ATTN_HEADS):
            softmax_head(h, visit, *cur)
            if nxt is not None:
                score_head(h, visit, *nxt, False)

    buf0 = (s0_ref, mx0_ref)
    buf1 = (s1_ref, mx1_ref)

    @pl.when(j == 0)
    def _():
        load_queries(0)
        for h in range(ATTN_HEADS):
            score_head(h, 0, *buf0, True)

    m_ref[...] = jnp.full(m_ref.shape, MASKED, jnp.float32)
    l_ref[...] = jnp.zeros_like(l_ref)
    acc_ref[...] = jnp.zeros_like(acc_ref)

    def visit_pair(i, carry):
        stage(2 * i, buf0, buf1)
        stage(2 * i + 1, buf1, buf0)
        return carry

    lax.fori_loop(0, j // 2, visit_pair, 0)

    @pl.when(j % 2 == 1)
    def _():
        stage(j - 1, buf0, buf1)

    def finish(cur):
        nxt = jnp.minimum(j + 1, nblk - 1)
        load_queries(nxt)
        for h in range(ATTN_HEADS):
            softmax_head(h, j, *cur)
            score_head(h, nxt, *buf0, True)
        conv_proj = _dot(yc_ref[0], w_ref[D_ATTN:, :])
        yat = jnp.concatenate(
            [acc_ref[h * HEAD_DIM:(h + 1) * HEAD_DIM, :] / l_ref[h, 0:1, :]
             for h in range(ATTN_HEADS)], axis=0)
        scale = lax.rsqrt(jnp.mean(yat * yat, axis=0, keepdims=True) + EPS)
        yant = ((yat * scale) * ag_ref[...]).astype(jnp.bfloat16)
        proj = lax.dot_general(yant, w_ref[:D_ATTN, :], (((0,), (0,)), ((), ())),
                               preferred_element_type=jnp.float32)
        o_ref[0] = x_ref[0] + conv_proj + proj

    @pl.when(j % 2 == 0)
    def _():
        finish(buf0)

    @pl.when(j % 2 == 1)
    def _():
        finish(buf1)


def _mix_attn(qt, k, vt, sel, yc, x, attn_g, w_out):
    B, S, D = x.shape
    T = SEQ_TILE
    nb = S // T
    tile = lambda n: pl.BlockSpec((1, T, n), lambda b, s: (b, s, 0))
    seq = lambda n: pl.BlockSpec((1, S, n), lambda b, s: (b, 0, 0))
    stat = pltpu.VMEM((ATTN_HEADS, SUBLANES, T), jnp.float32)
    return pl.pallas_call(
        _mix_attn_kernel,
        grid=(B, nb),
        in_specs=[pl.BlockSpec((1, nb, D_ATTN, T), lambda b, s: (b, 0, 0, 0)), seq(D_ATTN),
                  pl.BlockSpec((1, nb, D_ATTN, T), lambda b, s: (b, 0, 0, 0)),
                  pl.BlockSpec((1, nb * ATTN_HEADS, T), lambda b, s: (b, 0, s)),
                  tile(yc.shape[2]), tile(D), _resident(attn_g), _resident(w_out)],
        out_specs=tile(D),
        out_shape=jax.ShapeDtypeStruct((B, S, D), jnp.float32),
        scratch_shapes=[pltpu.VMEM((ATTN_HEADS * HEAD_PAIR, T), jnp.bfloat16),
                        pltpu.VMEM((ATTN_HEADS, T, T), jnp.float32),
                        stat,
                        pltpu.VMEM((ATTN_HEADS, T, T), jnp.float32),
                        stat, stat, stat,
                        pltpu.VMEM((D_ATTN, T), jnp.float32)],
        compiler_params=pltpu.CompilerParams(
            dimension_semantics=("parallel", "arbitrary"),
            vmem_limit_bytes=VMEM_LIMIT_BYTES),
        name="mix_attn",
    )(qt, k, vt, sel, yc, x, attn_g, w_out)


def _mem_attn_kernel(x_ref, mem_ref, g_ref, gm_ref, wq_ref, wkv_ref, wo_ref, o_ref,
                     k_ref, v_ref):
    d = x_ref.shape[2]
    hd = d // MEM_HEADS
    heads = [slice(i * hd, (i + 1) * hd) for i in range(MEM_HEADS)]

    m = mem_ref[0]
    hm = ((m * _rms_scale(m)) * gm_ref[...]).astype(jnp.bfloat16)
    k_ref[...] = _dot(hm, wkv_ref[:, :d]).astype(jnp.bfloat16)
    v_ref[...] = _dot(hm, wkv_ref[:, d:]).astype(jnp.bfloat16)

    def scores(r):
        x = x_ref[0, r:r + SEQ_TILE, :]
        h = ((x * _rms_scale(x)) * g_ref[...]).astype(jnp.bfloat16)
        q = (_dot(h, wq_ref[...]) * (hd ** -0.5 * LOG2E)).astype(jnp.bfloat16)
        return [_dot_nt(q[:, c], k_ref[:, c]) for c in heads]

    groups = list(range(0, x_ref.shape[1], SEQ_TILE))
    sc_next = scores(groups[0])
    for gi, r in enumerate(groups):
        sc = sc_next
        if gi + 1 < len(groups):
            sc_next = scores(groups[gi + 1])
        outs = []
        for s_h, c in zip(sc, heads):
            e = jnp.exp2(s_h - jnp.max(s_h, axis=-1, keepdims=True))
            o = _dot(e.astype(jnp.bfloat16), v_ref[:, c])
            outs.append((o / jnp.sum(e, axis=-1, keepdims=True)).astype(jnp.bfloat16))
        o_ref[0, r:r + SEQ_TILE, :] = (x_ref[0, r:r + SEQ_TILE, :]
                                       + _dot(jnp.concatenate(outs, axis=1), wo_ref[...]))


def _mem_attn(x, mem, g, g_mem, w_q, w_kv, w_o):
    B, S, D = x.shape
    M = mem.shape[1]
    seq = pl.BlockSpec((1, S, D), lambda b: (b, 0, 0))
    return pl.pallas_call(
        _mem_attn_kernel,
        grid=(B,),
        in_specs=[seq, pl.BlockSpec((1, M, D), lambda b: (b, 0, 0)),
                  _resident(g), _resident(g_mem), _resident(w_q), _resident(w_kv),
                  _resident(w_o)],
        out_specs=seq,
        out_shape=jax.ShapeDtypeStruct((B, S, D), jnp.float32),
        scratch_shapes=[pltpu.VMEM((M, D), jnp.bfloat16), pltpu.VMEM((M, D), jnp.bfloat16)],
        compiler_params=pltpu.CompilerParams(
            dimension_semantics=("parallel",),
            vmem_limit_bytes=VMEM_LIMIT_BYTES),
        name="mem_attn",
    )(x, mem, g, g_mem, w_q, w_kv, w_o)


def _ffn_kernel(x_ref, g_ref, wu_ref, cw_ref, wd_ref, fg_ref, o_ref, tail_ref, *, final_norm):
    s = pl.program_id(1)
    dff = cw_ref.shape[1]

    @pl.when(s == 0)
    def _():
        tail_ref[...] = jnp.zeros_like(tail_ref)

    tail = tail_ref[...]
    for r in range(0, x_ref.shape[1], SEQ_TILE):
        x = x_ref[0, r:r + SEQ_TILE, :]
        h = ((x * _rms_scale(x)) * g_ref[...]).astype(jnp.bfloat16)
        pre = _dot(h, wu_ref[:, :dff])
        gate = _causal_conv3(pre, tail, cw_ref[...])
        tail = pre[pre.shape[0] - SUBLANES:]
        u = _dot(h, wu_ref[:, dff:])
        act = (gate * (1.0 / (1.0 + jnp.exp(-gate))) * u).astype(jnp.bfloat16)
        y = x + _dot(act, wd_ref[...])
        if final_norm:
            y = (y * _rms_scale(y)) * fg_ref[...]
        o_ref[0, r:r + SEQ_TILE, :] = y
    tail_ref[...] = tail


def _ffn(x, g, w_up, conv_w, w_down, final_g, final_norm):
    B, S, D = x.shape
    T = ROW_TILE
    dff = conv_w.shape[1]
    tile = pl.BlockSpec((1, T, D), lambda b, s: (b, s, 0))
    return pl.pallas_call(
        functools.partial(_ffn_kernel, final_norm=final_norm),
        grid=(B, S // T),
        in_specs=[tile, _resident(g), _resident(w_up), _resident(conv_w),
                  _resident(w_down), _resident(final_g)],
        out_specs=tile,
        out_shape=jax.ShapeDtypeStruct((B, S, D), jnp.float32),
        scratch_shapes=[pltpu.VMEM((SUBLANES, dff), jnp.float32)],
        compiler_params=pltpu.CompilerParams(
            dimension_semantics=("parallel", "arbitrary"),
            vmem_limit_bytes=VMEM_LIMIT_BYTES),
        name="ffn",
    )(x, g, w_up, conv_w, w_down, final_g)


def kernel(x, mem, norm_mix_g, w_in_mix, conv_mix_w, attn_out_g, conv_out_g, w_out_mix, norm_mem_g, mem_kv_norm_g, w_mem_q, w_mem_kv, w_mem_o, norm_ffn_g, w_ffn_up, ffn_conv_w, w_ffn_down, final_norm_g):
    depth = w_in_mix.shape[0]
    bf = jnp.bfloat16
    row = lambda a: a.reshape(1, -1)
    for l in range(depth):
        w_in = w_in_mix[l].astype(bf)
        w_qvt = jnp.concatenate([w_in[:, :D_ATTN], w_in[:, 2 * D_ATTN:3 * D_ATTN]], axis=1).T
        qt, k, vt, yc, sel = _mix_in(x, row(norm_mix_g[l]), w_in, w_qvt,
                                     conv_mix_w[l], row(conv_out_g[l]))
        attn_g_cols = jnp.broadcast_to(attn_out_g[l][:, None], (D_ATTN, SEQ_TILE))
        x = _mix_attn(qt, k, vt, sel, yc, x, attn_g_cols, w_out_mix[l].astype(bf))
        x = _mem_attn(x, mem, row(norm_mem_g[l]), row(mem_kv_norm_g[l]),
                      w_mem_q[l].astype(bf), w_mem_kv[l].astype(bf), w_mem_o[l].astype(bf))
        x = _ffn(x, row(norm_ffn_g[l]), w_ffn_up[l].astype(bf), ffn_conv_w[l],
                 w_ffn_down[l].astype(bf), row(final_norm_g), final_norm=(l == depth - 1))
    return x
```

```python
import functools

import jax
import jax.numpy as jnp
from jax import lax
from jax.experimental import pallas as pl
from jax.experimental.pallas import tpu as pltpu

HEAD_DIM = 64
ATTN_HEADS = 8
D_ATTN = ATTN_HEADS * HEAD_DIM
MOBA_BLOCK = 256
MOBA_TOPK = 3
MEM_HEADS = 4
EPS = 1e-6

LANES = 128
SUBLANES = 8
VMEM_LIMIT_BYTES = 56 * 1024 * 1024

SEQ_TILE = MOBA_BLOCK
ROW_TILE = 4 * SEQ_TILE
HEAD_PAIR = 2 * HEAD_DIM
MASKED = -1e30
LOG2E = 1.4426950408889634

_NT = (((1,), (1,)), ((), ()))


def _dot(a, b):
    return jnp.dot(a, b, preferred_element_type=jnp.float32)


def _dot_nt(a, b):
    return lax.dot_general(a, b, _NT, preferred_element_type=jnp.float32)


def _rms_scale(x):
    return lax.rsqrt(jnp.mean(x * x, axis=-1, keepdims=True) + EPS)


def _split_bf16(a):
    hi = a.astype(jnp.bfloat16)
    lo = (a - hi.astype(jnp.float32)).astype(jnp.bfloat16)
    return hi, lo


def _causal_conv3(z, tail, w):
    rows = lax.broadcasted_iota(jnp.int32, (SUBLANES, z.shape[1]), 0)
    z1 = pltpu.roll(z, 1, axis=0)
    z2 = pltpu.roll(z, 2, axis=0)
    t1 = pltpu.roll(tail, 1, axis=0)
    t2 = pltpu.roll(tail, 2, axis=0)
    z1 = jnp.concatenate([jnp.where(rows < 1, t1, z1[:SUBLANES]), z1[SUBLANES:]], axis=0)
    z2 = jnp.concatenate([jnp.where(rows < 2, t2, z2[:SUBLANES]), z2[SUBLANES:]], axis=0)
    return z2 * w[0:1] + z1 * w[1:2] + z * w[2:3]


def _resident(a):
    return pl.BlockSpec(a.shape, lambda *_: (0,) * a.ndim, pipeline_mode=pl.Buffered(1))


def _mix_in_kernel(x_ref, g_ref, w_ref, wqvt_ref, cw_ref, cg_ref,
                   qt_ref, k_ref, vt_ref, yc_ref, sel_ref,
                   km_ref, tail_ref):
    s = pl.program_id(1)
    da = D_ATTN
    dc = cw_ref.shape[1]
    nrow = km_ref.shape[0]
    nblk = nrow // ATTN_HEADS
    T = SEQ_TILE
    groups = x_ref.shape[1] // T

    @pl.when(s == 0)
    def _():
        km_ref[...] = jnp.zeros_like(km_ref)
        tail_ref[...] = jnp.zeros_like(tail_ref)

    row_head = lax.broadcasted_iota(jnp.int32, (ATTN_HEADS, da), 0)
    col_head = lax.broadcasted_iota(jnp.int32, (ATTN_HEADS, da), 1) // HEAD_DIM
    tail = tail_ref[...]

    for gi in range(groups):
        blk = s * groups + gi
        rows = slice(gi * T, (gi + 1) * T)
        x = x_ref[0, rows, :]
        h = ((x * _rms_scale(x)) * g_ref[...]).astype(jnp.bfloat16)

        o = 3 * da
        cgate = _dot(h, w_ref[:, o + dc:o + 2 * dc])
        u = _dot(h, w_ref[:, o + 2 * dc:o + 3 * dc])
        z = cgate * u
        conv = _causal_conv3(z, tail, cw_ref[...])
        tail = z[T - SUBLANES:]
        y = _dot(h, w_ref[:, o:o + dc]) * conv
        yc_ref[0, rows, :] = ((y * _rms_scale(y)) * cg_ref[...]).astype(jnp.bfloat16)

        qt = _dot_nt(wqvt_ref[:da, :], h)
        vt_ref[0, gi] = _dot_nt(wqvt_ref[da:, :], h).astype(jnp.bfloat16)
        qt_ref[0, gi] = (qt * (HEAD_DIM ** -0.5 * LOG2E)).astype(jnp.bfloat16)

        qh, ql = _split_bf16(qt)
        kmh, kml = _split_bf16(km_ref[...])
        gate = _dot(jnp.concatenate([kmh, kml], axis=0), qh)
        gate = gate[:nrow] + gate[nrow:] + _dot(kmh, ql)
        g = [jnp.where(n < blk, gate[n * ATTN_HEADS:(n + 1) * ATTN_HEADS], -jnp.inf)
             for n in range(nblk)]
        bias = []
        for n in range(nblk):
            rank = jnp.zeros(g[n].shape, jnp.int32)
            for o in range(nblk):
                if o != n:
                    beats = (g[o] >= g[n]) if o < n else (g[o] > g[n])
                    rank = rank + jnp.where(beats, 1, 0)
            chosen = jnp.logical_and(n < blk, rank < MOBA_TOPK)
            bias.append(jnp.where(chosen, 0.0, MASKED).astype(jnp.float32))
        sel_ref[0, :, rows] = jnp.concatenate(bias, axis=0)

        k = _dot(h, w_ref[:, da:2 * da])
        k_ref[0, rows, :] = k.astype(jnp.bfloat16)

        kmean = jnp.mean(k, axis=0, keepdims=True)
        base = pl.multiple_of(blk * ATTN_HEADS, ATTN_HEADS)
        km_ref[pl.ds(base, ATTN_HEADS), :] = jnp.where(row_head == col_head, kmean, 0.0)

    tail_ref[...] = tail


def _mix_in(x, g, w_in, w_qvt, conv_w, conv_g):
    B, S, D = x.shape
    T = 2 * ROW_TILE
    nb = S // SEQ_TILE
    dc = conv_w.shape[1]
    tile = lambda n: pl.BlockSpec((1, T, n), lambda b, s: (b, s, 0))
    blocked = pl.BlockSpec((1, T // SEQ_TILE, D_ATTN, SEQ_TILE), lambda b, s: (b, s, 0, 0))
    bf = jnp.bfloat16
    return pl.pallas_call(
        _mix_in_kernel,
        grid=(B, S // T),
        in_specs=[tile(D), _resident(g), _resident(w_in), _resident(w_qvt),
                  _resident(conv_w), _resident(conv_g)],
        out_specs=[blocked,
                   tile(D_ATTN),
                   blocked,
                   tile(dc),
                   pl.BlockSpec((1, nb * ATTN_HEADS, T), lambda b, s: (b, 0, s))],
        out_shape=[jax.ShapeDtypeStruct((B, nb, D_ATTN, SEQ_TILE), bf),
                   jax.ShapeDtypeStruct((B, S, D_ATTN), bf),
                   jax.ShapeDtypeStruct((B, nb, D_ATTN, SEQ_TILE), bf),
                   jax.ShapeDtypeStruct((B, S, dc), bf),
                   jax.ShapeDtypeStruct((B, nb * ATTN_HEADS, S), jnp.float32)],
        scratch_shapes=[pltpu.VMEM((nb * ATTN_HEADS, D_ATTN), jnp.float32),
                        pltpu.VMEM((SUBLANES, dc), jnp.float32)],
        compiler_params=pltpu.CompilerParams(
            dimension_semantics=("parallel", "arbitrary"),
            vmem_limit_bytes=VMEM_LIMIT_BYTES),
        name="mix_in",
    )(x, g, w_in, w_qvt, conv_w, conv_g)


def _mix_attn_kernel(qt_ref, k_ref, vt_ref, sel_ref, yc_ref, x_ref, ag_ref, w_ref,
                     o_ref, qm_ref, s0_ref, mx0_ref, s1_ref, mx1_ref, m_ref, l_ref, acc_ref):
    j = pl.program_id(1)
    T = SEQ_TILE
    nblk = vt_ref.shape[1]

    zeros = jnp.zeros((HEAD_DIM, T), jnp.bfloat16)
    ones = jnp.ones((2 * SUBLANES, T), jnp.bfloat16)

    def load_queries(tile):
        for h in range(ATTN_HEADS):
            qh = qt_ref[0, tile, h * HEAD_DIM:(h + 1) * HEAD_DIM, :]
            qm_ref[h * HEAD_PAIR:(h + 1) * HEAD_PAIR, :] = jnp.concatenate(
                [qh, zeros] if h % 2 == 0 else [zeros, qh], axis=0)

    key = lax.broadcasted_iota(jnp.int32, (T, T), 0)
    qry = lax.broadcasted_iota(jnp.int32, (T, T), 1)
    causal = key <= qry

    def score_head(h, blk, s_ref, mx_ref, own):
        p = h // 2
        start = pl.multiple_of(blk * T, T)
        kp = k_ref[0, pl.ds(start, T), p * HEAD_PAIR:(p + 1) * HEAD_PAIR]
        sc = _dot(kp, qm_ref[h * HEAD_PAIR:(h + 1) * HEAD_PAIR, :])
        if own:
            sc = jnp.where(causal, sc, MASKED)
        s_ref[h] = sc
        mx = jnp.max(sc.reshape(T // SUBLANES, SUBLANES, T), axis=0)
        for sh in (4, 2, 1):
            mx = jnp.maximum(mx, pltpu.roll(mx, sh, axis=0))
        mx_ref[h] = mx

    def softmax_head(h, visit, s_ref, mx_ref):
        rows = slice(h * HEAD_DIM, (h + 1) * HEAD_DIM)
        blk = jnp.where(visit == 0, j, visit - 1)
        sel_row = jnp.maximum(visit - 1, 0) * ATTN_HEADS + h
        bias = jnp.broadcast_to(sel_ref[0, pl.ds(sel_row, 1), :], (SUBLANES, T))
        bias = jnp.where(visit == 0, 0.0, bias)
        m_old = m_ref[h]
        m_new = jnp.maximum(m_old, mx_ref[h] + bias)
        alpha = jnp.exp2(m_old - m_new)
        shift = m_new - bias
        e = jnp.exp2(s_ref[h].reshape(T // SUBLANES, SUBLANES, T) - shift[None])
        lhs = jnp.concatenate([vt_ref[0, blk, rows, :], ones], axis=0)
        pv = _dot(lhs, e.reshape(T, T).astype(jnp.bfloat16))
        m_ref[h] = m_new
        l_ref[h] = alpha * l_ref[h] + pv[HEAD_DIM:HEAD_DIM + SUBLANES]
        acc3 = acc_ref[rows, :].reshape(HEAD_DIM // SUBLANES, SUBLANES, T)
        acc3 = acc3 * alpha[None] + pv[:HEAD_DIM].reshape(HEAD_DIM // SUBLANES, SUBLANES, T)
        acc_ref[rows, :] = acc3.reshape(HEAD_DIM, T)

    def stage(visit, cur, nxt):
        for h in range(ATTN_HEADS):
            softmax_head(h, visit, *cur)
            if nxt is not None:
                score_head(h, visit, *nxt, False)

    buf0 = (s0_ref, mx0_ref)
    buf1 = (s1_ref, mx1_ref)

    @pl.when(j == 0)
    def _():
        load_queries(0)
        for h in range(ATTN_HEADS):
            score_head(h, 0, *buf0, True)

    m_ref[...] = jnp.full(m_ref.shape, MASKED, jnp.float32)
    l_ref[...] = jnp.zeros_like(l_ref)
    acc_ref[...] = jnp.zeros_like(acc_ref)

    def visit_pair(i, carry):
        stage(2 * i, buf0, buf1)
        stage(2 * i + 1, buf1, buf0)
        return carry

    lax.fori_loop(0, j // 2, visit_pair, 0)

    @pl.when(j % 2 == 1)
    def _():
        stage(j - 1, buf0, buf1)

    def finish(cur):
        nxt = jnp.minimum(j + 1, nblk - 1)
        load_queries(nxt)
        for h in range(ATTN_HEADS):
            softmax_head(h, j, *cur)
            score_head(h, nxt, *buf0, True)
        conv_proj = _dot(yc_ref[0], w_ref[D_ATTN:, :])
        yat = jnp.concatenate(
            [acc_ref[h * HEAD_DIM:(h + 1) * HEAD_DIM, :] / l_ref[h, 0:1, :]
             for h in range(ATTN_HEADS)], axis=0)
        scale = lax.rsqrt(jnp.mean(yat * yat, axis=0, keepdims=True) + EPS)
        yant = ((yat * scale) * ag_ref[...]).astype(jnp.bfloat16)
        proj = lax.dot_general(yant, w_ref[:D_ATTN, :], (((0,), (0,)), ((), ())),
                               preferred_element_type=jnp.float32)
        o_ref[0] = x_ref[0] + conv_proj + proj

    @pl.when(j % 2 == 0)
    def _():
        finish(buf0)

    @pl.when(j % 2 == 1)
    def _():
        finish(buf1)


def _mix_attn(qt, k, vt, sel, yc, x, attn_g, w_out):
    B, S, D = x.shape
    T = SEQ_TILE
    nb = S // T
    tile = lambda n: pl.BlockSpec((1, T, n), lambda b, s: (b, s, 0))
    seq = lambda n: pl.BlockSpec((1, S, n), lambda b, s: (b, 0, 0))
    stat = pltpu.VMEM((ATTN_HEADS, SUBLANES, T), jnp.float32)
    return pl.pallas_call(
        _mix_attn_kernel,
        grid=(B, nb),
        in_specs=[pl.BlockSpec((1, nb, D_ATTN, T), lambda b, s: (b, 0, 0, 0)), seq(D_ATTN),
                  pl.BlockSpec((1, nb, D_ATTN, T), lambda b, s: (b, 0, 0, 0)),
                  pl.BlockSpec((1, nb * ATTN_HEADS, T), lambda b, s: (b, 0, s)),
                  tile(yc.shape[2]), tile(D), _resident(attn_g), _resident(w_out)],
        out_specs=tile(D),
        out_shape=jax.ShapeDtypeStruct((B, S, D), jnp.float32),
        scratch_shapes=[pltpu.VMEM((ATTN_HEADS * HEAD_PAIR, T), jnp.bfloat16),
                        pltpu.VMEM((ATTN_HEADS, T, T), jnp.float32),
                        stat,
                        pltpu.VMEM((ATTN_HEADS, T, T), jnp.float32),
                        stat, stat, stat,
                        pltpu.VMEM((D_ATTN, T), jnp.float32)],
        compiler_params=pltpu.CompilerParams(
            dimension_semantics=("parallel", "arbitrary"),
            vmem_limit_bytes=VMEM_LIMIT_BYTES),
        name="mix_attn",
    )(qt, k, vt, sel, yc, x, attn_g, w_out)


def _mem_attn_kernel(x_ref, mem_ref, g_ref, gm_ref, wq_ref, wkv_ref, wo_ref, o_ref,
                     k_ref, v_ref):
    d = x_ref.shape[2]
    hd = d // MEM_HEADS
    heads = [slice(i * hd, (i + 1) * hd) for i in range(MEM_HEADS)]

    m = mem_ref[0]
    hm = ((m * _rms_scale(m)) * gm_ref[...]).astype(jnp.bfloat16)
    k_ref[...] = _dot(hm, wkv_ref[:, :d]).astype(jnp.bfloat16)
    v_ref[...] = _dot(hm, wkv_ref[:, d:]).astype(jnp.bfloat16)

    def scores(r):
        x = x_ref[0, r:r + SEQ_TILE, :]
        h = ((x * _rms_scale(x)) * g_ref[...]).astype(jnp.bfloat16)
        q = (_dot(h, wq_ref[...]) * (hd ** -0.5 * LOG2E)).astype(jnp.bfloat16)
        return [_dot_nt(q[:, c], k_ref[:, c]) for c in heads]

    groups = list(range(0, x_ref.shape[1], SEQ_TILE))
    sc_next = scores(groups[0])
    for gi, r in enumerate(groups):
        sc = sc_next
        if gi + 1 < len(groups):
            sc_next = scores(groups[gi + 1])
        outs = []
        for s_h, c in zip(sc, heads):
            e = jnp.exp2(s_h - jnp.max(s_h, axis=-1, keepdims=True))
            o = _dot(e.astype(jnp.bfloat16), v_ref[:, c])
            outs.append((o / jnp.sum(e, axis=-1, keepdims=True)).astype(jnp.bfloat16))
        o_ref[0, r:r + SEQ_TILE, :] = (x_ref[0, r:r + SEQ_TILE, :]
                                       + _dot(jnp.concatenate(outs, axis=1), wo_ref[...]))


def _mem_attn(x, mem, g, g_mem, w_q, w_kv, w_o):
    B, S, D = x.shape
    M = mem.shape[1]
    seq = pl.BlockSpec((1, S, D), lambda b: (b, 0, 0))
    return pl.pallas_call(
        _mem_attn_kernel,
        grid=(B,),
        in_specs=[seq, pl.BlockSpec((1, M, D), lambda b: (b, 0, 0)),
                  _resident(g), _resident(g_mem), _resident(w_q), _resident(w_kv),
                  _resident(w_o)],
        out_specs=seq,
        out_shape=jax.ShapeDtypeStruct((B, S, D), jnp.float32),
        scratch_shapes=[pltpu.VMEM((M, D), jnp.bfloat16), pltpu.VMEM((M, D), jnp.bfloat16)],
        compiler_params=pltpu.CompilerParams(
            dimension_semantics=("parallel",),
            vmem_limit_bytes=VMEM_LIMIT_BYTES),
        name="mem_attn",
    )(x, mem, g, g_mem, w_q, w_kv, w_o)


def _ffn_kernel(x_ref, g_ref, wu_ref, cw_ref, wd_ref, fg_ref, o_ref, tail_ref, *, final_norm):
    s = pl.program_id(1)
    dff = cw_ref.shape[1]

    @pl.when(s == 0)
    def _():
        tail_ref[...] = jnp.zeros_like(tail_ref)

    def up(r):
        x = x_ref[0, r:r + SEQ_TILE, :]
        h = ((x * _rms_scale(x)) * g_ref[...]).astype(jnp.bfloat16)
        return _dot(h, wu_ref[:, :dff]), _dot(h, wu_ref[:, dff:])

    tail = tail_ref[...]
    groups = list(range(0, x_ref.shape[1], SEQ_TILE))
    nxt = up(groups[0])
    for gi, r in enumerate(groups):
        pre, u = nxt
        if gi + 1 < len(groups):
            nxt = up(groups[gi + 1])
        gate = _causal_conv3(pre, tail, cw_ref[...])
        tail = pre[pre.shape[0] - SUBLANES:]
        act = (gate * (1.0 / (1.0 + jnp.exp(-gate))) * u).astype(jnp.bfloat16)
        y = x_ref[0, r:r + SEQ_TILE, :] + _dot(act, wd_ref[...])
        if final_norm:
            y = (y * _rms_scale(y)) * fg_ref[...]
        o_ref[0, r:r + SEQ_TILE, :] = y
    tail_ref[...] = tail


def _ffn(x, g, w_up, conv_w, w_down, final_g, final_norm):
    B, S, D = x.shape
    T = ROW_TILE
    dff = conv_w.shape[1]
    tile = pl.BlockSpec((1, T, D), lambda b, s: (b, s, 0))
    return pl.pallas_call(
        functools.partial(_ffn_kernel, final_norm=final_norm),
        grid=(B, S // T),
        in_specs=[tile, _resident(g), _resident(w_up), _resident(conv_w),
                  _resident(w_down), _resident(final_g)],
        out_specs=tile,
        out_shape=jax.ShapeDtypeStruct((B, S, D), jnp.float32),
        scratch_shapes=[pltpu.VMEM((SUBLANES, dff), jnp.float32)],
        compiler_params=pltpu.CompilerParams(
            dimension_semantics=("parallel", "arbitrary"),
            vmem_limit_bytes=VMEM_LIMIT_BYTES),
        name="ffn",
    )(x, g, w_up, conv_w, w_down, final_g)


def kernel(x, mem, norm_mix_g, w_in_mix, conv_mix_w, attn_out_g, conv_out_g, w_out_mix, norm_mem_g, mem_kv_norm_g, w_mem_q, w_mem_kv, w_mem_o, norm_ffn_g, w_ffn_up, ffn_conv_w, w_ffn_down, final_norm_g):
    depth = w_in_mix.shape[0]
    bf = jnp.bfloat16
    row = lambda a: a.reshape(1, -1)
    for l in range(depth):
        w_in = w_in_mix[l].astype(bf)
        w_qvt = jnp.concatenate([w_in[:, :D_ATTN], w_in[:, 2 * D_ATTN:3 * D_ATTN]], axis=1).T
        qt, k, vt, yc, sel = _mix_in(x, row(norm_mix_g[l]), w_in, w_qvt,
                                     conv_mix_w[l], row(conv_out_g[l]))
        attn_g_cols = jnp.broadcast_to(attn_out_g[l][:, None], (D_ATTN, SEQ_TILE))
        x = _mix_attn(qt, k, vt, sel, yc, x, attn_g_cols, w_out_mix[l].astype(bf))
        x = _mem_attn(x, mem, row(norm_mem_g[l]), row(mem_kv_norm_g[l]),
                      w_mem_q[l].astype(bf), w_mem_kv[l].astype(bf), w_mem_o[l].astype(bf))
        x = _ffn(x, row(norm_ffn_g[l]), w_ffn_up[l].astype(bf), ffn_conv_w[l],
                 w_ffn_down[l].astype(bf), row(final_norm_g), final_norm=(l == depth - 1))
    return x
```
